```python
import math
import jax, jax.numpy as jnp
from jax import lax
import numpy as np

D_MODEL = 1024
BATCH = 2
SEQ = 8192
DEPTH = 1
DEC_BATCH = 8
DEC_SEQ = 4096
PAST_LEN = 128

SGU_WIDTH = D_MODEL
SGU_GROUPS = 8
SGU_GROUP_DIM = SGU_WIDTH // SGU_GROUPS
CHUNK = 128
ATT_HEADS = 8
ATT_HEAD_DIM = 64
ATT_VDIM = 2 * ATT_HEAD_DIM
ATT_QK_WIDTH = ATT_HEADS * 2 * ATT_HEAD_DIM
ATT_WIDTH = ATT_HEADS * ATT_VDIM
Q_BLOCK = 128
IN_COLS = 2 * SGU_WIDTH + 2 * ATT_QK_WIDTH + ATT_WIDTH + 2 * D_MODEL
N_EXPERTS = 32
TOP_K = 4
D_FF = D_MODEL
SWIGLU_LIMIT = 7.0
SWIGLU_ALPHA = 1.702
ROUTE_BLOCK = 256
RMS_EPS = 1e-5

kernel_name = "hybrid_gmlp_diffattn_moe_encoder"


def rmsnorm(x, g):
    xf = x.astype(jnp.float32)
    r = xf * lax.rsqrt(jnp.mean(xf * xf, axis=-1, keepdims=True) + RMS_EPS)
    return (r * g.astype(jnp.float32)).astype(x.dtype)


def alibi_slopes(n):
    return jnp.exp2(-8.0 * jnp.arange(1, n + 1, dtype=jnp.float32) / n)


def spatial_gating(uv, g_sgu, w_s, b_s):
    u, v = jnp.split(jax.nn.gelu(uv), 2, axis=-1)
    v = rmsnorm(v, g_sgu)
    B, S = v.shape[0], v.shape[1]
    vc = v.reshape(B, S // CHUNK, CHUNK, SGU_GROUPS, SGU_GROUP_DIM)
    vm = jnp.einsum('gts,bnsgc->bntgc', w_s, vc) + b_s.T[None, None, :, :, None]
    return u * vm.reshape(B, S, SGU_WIDTH)


def diff_attention(q, k, v, lam, g_subln, lambda_init):
    B, S = q.shape[0], q.shape[1]
    nqb = S // Q_BLOCK
    scale = ATT_HEAD_DIM ** -0.5
    slopes = alibi_slopes(ATT_HEADS)
    kpos = jnp.arange(S, dtype=jnp.float32)
    q_blocks = q.reshape(B, nqb, Q_BLOCK, ATT_HEADS, 2, ATT_HEAD_DIM).swapaxes(0, 1)
    starts = jnp.arange(nqb, dtype=jnp.int32) * Q_BLOCK

    def one_block(args):
        qb, start = args
        s = jnp.einsum('bqhcd,bkhcd->bhcqk', qb, k).astype(jnp.float32) * scale
        qpos = start.astype(jnp.float32) + jnp.arange(Q_BLOCK, dtype=jnp.float32)
        dist = jnp.abs(qpos[:, None] - kpos[None, :])
        s = s - slopes[None, :, None, None, None] * dist[None, None, None]
        p = jax.nn.softmax(s, axis=-1)
        a = p[:, :, 0] - lam * p[:, :, 1]
        return jnp.einsum('bhqk,bkhe->bqhe', a.astype(v.dtype), v)

    o = lax.map(one_block, (q_blocks, starts))
    o = o.swapaxes(0, 1).reshape(B, S, ATT_HEADS, ATT_VDIM)
    o = rmsnorm(o, g_subln) * (1.0 - lambda_init)
    return o.reshape(B, S, ATT_WIDTH)


def moe_ffn(h, w_router, b_router, w_gu, b_gu, w_down, b_down):
    T = h.shape[0]
    n_rows = T * TOP_K
    logits = (h @ w_router + b_router).astype(jnp.float32)
    top_val, top_idx = lax.top_k(logits, TOP_K)
    gates = jax.nn.softmax(top_val, axis=-1)
    flat_e = top_idx.reshape(-1).astype(jnp.int32)
    flat_gate = gates.reshape(-1)
    flat_tok = jnp.arange(n_rows, dtype=jnp.int32) // TOP_K
    order = jnp.argsort(flat_e)
    sorted_e = flat_e[order]
    counts = jnp.bincount(flat_e, length=N_EXPERTS).astype(jnp.int32)
    padded_counts = (counts + ROUTE_BLOCK - 1) // ROUTE_BLOCK * ROUTE_BLOCK
    group_start = jnp.cumsum(counts) - counts
    padded_end = jnp.cumsum(padded_counts)
    padded_start = padded_end - padded_counts
    rank = jnp.arange(n_rows, dtype=jnp.int32) - group_start[sorted_e]
    dest = padded_start[sorted_e] + rank
    n_blocks = -(-n_rows // ROUTE_BLOCK) + N_EXPERTS
    P = n_blocks * ROUTE_BLOCK
    row_tok = jnp.full((P,), T, dtype=jnp.int32).at[dest].set(flat_tok[order])
    row_gate = jnp.zeros((P,), jnp.float32).at[dest].set(flat_gate[order])
    block_start = jnp.arange(n_blocks, dtype=jnp.int32) * ROUTE_BLOCK
    block_expert = jnp.minimum(jnp.searchsorted(padded_end, block_start, side='right'),
                               N_EXPERTS - 1).astype(jnp.int32)
    h_pad = jnp.concatenate([h, jnp.zeros((1, h.shape[1]), h.dtype)], axis=0)
    xs = h_pad[row_tok].reshape(n_blocks, ROUTE_BLOCK, h.shape[1])

    def expert_block(args):
        xb, e = args
        gu = xb @ w_gu[e] + b_gu[e]
        glu, lin = jnp.split(gu, 2, axis=-1)
        glu = jnp.minimum(glu, SWIGLU_LIMIT)
        lin = jnp.clip(lin, -SWIGLU_LIMIT, SWIGLU_LIMIT)
        act = glu * jax.nn.sigmoid(SWIGLU_ALPHA * glu) * (lin + 1.0)
        return act @ w_down[e] + b_down[e]

    y_rows = lax.map(expert_block, (xs, block_expert)).reshape(P, h.shape[1])
    y_rows = y_rows * row_gate.astype(h.dtype)[:, None]
    y = jnp.zeros((T + 1, h.shape[1]), h.dtype).at[row_tok].add(y_rows)
    return y[:T]


def encoder_forward(x, g_mix, w_in, g_sgu, w_spatial, b_spatial, lambda_q1, lambda_k1,
                    lambda_q2, lambda_k2, g_subln, w_branch_sgu, w_branch_attn, w_out,
                    g_ffn, w_router, b_router, w_gate_up, b_gate_up, w_down, b_down, g_final):
    B, S, D = x.shape
    for l in range(DEPTH):
        lambda_init = 0.8 - 0.6 * math.exp(-0.3 * l)
        xn = rmsnorm(x, g_mix[l])
        proj = xn @ w_in[l]
        c0 = 2 * SGU_WIDTH
        c1 = c0 + ATT_QK_WIDTH
        c2 = c1 + ATT_QK_WIDTH
        c3 = c2 + ATT_WIDTH
        uv, q, k, v, gate_logits = jnp.split(proj, [c0, c1, c2, c3], axis=-1)
        sgu = spatial_gating(uv, g_sgu[l], w_spatial[l], b_spatial[l])
        lam = (jnp.exp(jnp.sum(lambda_q1[l].astype(jnp.float32) * lambda_k1[l].astype(jnp.float32)))
               - jnp.exp(jnp.sum(lambda_q2[l].astype(jnp.float32) * lambda_k2[l].astype(jnp.float32)))
               + lambda_init)
        att = diff_attention(q.reshape(B, S, ATT_HEADS, 2, ATT_HEAD_DIM),
                             k.reshape(B, S, ATT_HEADS, 2, ATT_HEAD_DIM),
                             v.reshape(B, S, ATT_HEADS, ATT_VDIM),
                             lam, g_subln[l], lambda_init)
        gate_sgu, gate_att = jnp.split(jax.nn.sigmoid(gate_logits), 2, axis=-1)
        merged = gate_sgu * (sgu @ w_branch_sgu[l]) + gate_att * (att @ w_branch_attn[l])
        x = x + merged @ w_out[l]
        hn = rmsnorm(x, g_ffn[l]).reshape(B * S, D)
        x = x + moe_ffn(hn, w_router[l], b_router[l], w_gate_up[l], b_gate_up[l],
                        w_down[l], b_down[l]).reshape(B, S, D)
    return rmsnorm(x, g_final)


def setup_inputs(seed: int = 0) -> dict:
    key = jax.random.key(seed)
    ks = jax.random.split(key, 24)
    f32 = jnp.float32
    L, D = DEPTH, D_MODEL
    nrm = lambda k, shape: jax.random.normal(k, shape, f32)
    return {
        "x_prompt": nrm(ks[0], (BATCH, SEQ, D)),
        "x_sample": nrm(ks[1], (DEC_BATCH, DEC_SEQ, D)),
        "g_mix": 1.0 + 0.01 * nrm(ks[2], (L, D)),
        "w_in": nrm(ks[3], (L, D, IN_COLS)) * D ** -0.5,
        "g_sgu": 1.0 + 0.01 * nrm(ks[4], (L, SGU_WIDTH)),
        "w_spatial": nrm(ks[5], (L, SGU_GROUPS, CHUNK, CHUNK)) * CHUNK ** -0.5,
        "b_spatial": 1.0 + 0.1 * nrm(ks[6], (L, SGU_GROUPS, CHUNK)),
        "lambda_q1": 0.1 * nrm(ks[7], (L, ATT_HEAD_DIM)),
        "lambda_k1": 0.1 * nrm(ks[8], (L, ATT_HEAD_DIM)),
        "lambda_q2": 0.1 * nrm(ks[9], (L, ATT_HEAD_DIM)),
        "lambda_k2": 0.1 * nrm(ks[10], (L, ATT_HEAD_DIM)),
        "g_subln": 1.0 + 0.01 * nrm(ks[11], (L, ATT_VDIM)),
        "w_branch_sgu": nrm(ks[12], (L, SGU_WIDTH, D)) * SGU_WIDTH ** -0.5,
        "w_branch_attn": nrm(ks[13], (L, ATT_WIDTH, D)) * ATT_WIDTH ** -0.5,
        "w_out": nrm(ks[14], (L, D, D)) * D ** -0.5,
        "g_ffn": 1.0 + 0.01 * nrm(ks[15], (L, D)),
        "w_router": nrm(ks[16], (L, D, N_EXPERTS)) * D ** -0.5,
        "b_router": 0.01 * nrm(ks[17], (L, N_EXPERTS)),
        "w_gate_up": nrm(ks[18], (L, N_EXPERTS, D, 2 * D_FF)) * D ** -0.5,
        "b_gate_up": 0.01 * nrm(ks[19], (L, N_EXPERTS, 2 * D_FF)),
        "w_down": nrm(ks[20], (L, N_EXPERTS, D_FF, D)) * D_FF ** -0.5,
        "b_down": 0.01 * nrm(ks[21], (L, N_EXPERTS, D)),
        "g_final": 1.0 + 0.01 * nrm(ks[22], (D,)),
    }


def reference(x_prompt, x_sample, g_mix, w_in, g_sgu, w_spatial, b_spatial, lambda_q1, lambda_k1,
              lambda_q2, lambda_k2, g_subln, w_branch_sgu, w_branch_attn, w_out, g_ffn,
              w_router, b_router, w_gate_up, b_gate_up, w_down, b_down, g_final):
    y_prompt = encoder_forward(x_prompt, g_mix, w_in, g_sgu, w_spatial, b_spatial, lambda_q1,
                               lambda_k1, lambda_q2, lambda_k2, g_subln, w_branch_sgu,
                               w_branch_attn, w_out, g_ffn, w_router, b_router, w_gate_up,
                               b_gate_up, w_down, b_down, g_final)
    y_sample = encoder_forward(x_sample, g_mix, w_in, g_sgu, w_spatial, b_spatial, lambda_q1,
                               lambda_k1, lambda_q2, lambda_k2, g_subln, w_branch_sgu,
                               w_branch_attn, w_out, g_ffn, w_router, b_router, w_gate_up,
                               b_gate_up, w_down, b_down, g_final)
    return (y_prompt, y_sample)
```

```python
import functools
import math

import jax
import jax.numpy as jnp
from jax import lax
from jax.experimental import pallas as pl
from jax.experimental.pallas import tpu as pltpu

D_MODEL = 1024
SGU_GROUPS = 8
CHUNK = 128
ATT_HEADS = 8
ATT_HEAD_DIM = 64
ATT_VDIM = 2 * ATT_HEAD_DIM
N_EXPERTS = 32
TOP_K = 4
D_FF = D_MODEL
SWIGLU_LIMIT = 7.0
SWIGLU_ALPHA = 1.702
RMS_EPS = 1e-5
LAMBDA_INIT = 0.8 - 0.6 * math.exp(-0.3 * 0)
LOG2E = 1.4426950408889634

COL_U, COL_V, COL_Q, COL_K, COL_VA, COL_GS, COL_GA = range(7)
IN_COLS = 7 * D_MODEL
LANES = 128
HEAD_BLOCKS = D_MODEL // LANES

VMEM_LIMIT_BYTES = 56 * 1024 * 1024

F32 = jnp.float32
BF16 = jnp.bfloat16


def _tiles(batch, seq):
    tokens = batch * seq
    return dict(
        tm_in=min(1024, tokens),
        tq=min(256, seq),
        tk=min(512, seq),
        tm_mix=min(512, tokens),
    )


ROUTE_BLOCK = 256
TOKEN_TILE = 256


def _gelu_tanh(x):
    return 0.5 * x * (1.0 + jnp.tanh(math.sqrt(2.0 / math.pi) * (x + 0.044715 * (x * x * x))))


def _rms(x, g):
    r = x * lax.rsqrt(jnp.mean(x * x, axis=-1, keepdims=True) + RMS_EPS)
    return r * g


def _inproj_kernel(x_ref, gmix_ref, w_ref, gsgu_ref, o_ref, xn_ref):
    j = pl.program_id(1)

    @pl.when(j == 0)
    def _():
        xn_ref[...] = _rms(x_ref[...], gmix_ref[...]).astype(BF16)

    y = jnp.dot(xn_ref[...], w_ref[...], preferred_element_type=F32)

    @pl.when(j == COL_U)
    def _():
        o_ref[...] = _gelu_tanh(y).astype(BF16)

    @pl.when(j == COL_V)
    def _():
        o_ref[...] = _rms(_gelu_tanh(y), gsgu_ref[...]).astype(BF16)

    @pl.when(jnp.logical_and(j >= COL_Q, j <= COL_VA))
    def _():
        o_ref[...] = y.astype(BF16)

    @pl.when(j >= COL_GS)
    def _():
        o_ref[...] = jax.nn.sigmoid(y).astype(BF16)


def _inproj(x2, g_mix, w_in, g_sgu, tm):
    tokens = x2.shape[0]
    return pl.pallas_call(
        _inproj_kernel,
        grid=(tokens // tm, IN_COLS // D_MODEL),
        in_specs=[
            pl.BlockSpec((tm, D_MODEL), lambda i, j: (i, 0)),
            pl.BlockSpec((1, D_MODEL), lambda i, j: (0, 0)),
            pl.BlockSpec((D_MODEL, D_MODEL), lambda i, j: (0, j)),
            pl.BlockSpec((1, D_MODEL), lambda i, j: (0, 0)),
        ],
        out_specs=pl.BlockSpec((tm, D_MODEL), lambda i, j: (i, j)),
        out_shape=jax.ShapeDtypeStruct((tokens, IN_COLS), BF16),
        scratch_shapes=[pltpu.VMEM((tm, D_MODEL), BF16)],
        compiler_params=pltpu.CompilerParams(
            dimension_semantics=("arbitrary", "arbitrary"),
            vmem_limit_bytes=VMEM_LIMIT_BYTES),
        name="inproj",
    )(x2, g_mix, w_in, g_sgu)


def _attn_kernel(sc_ref, q_ref, k_ref, v_ref, g_ref, o_ref,
                 qs_ref, d0_ref, m_ref, l_ref, acc_ref, *, tq, tk, seq):
    h = pl.program_id(1)
    qi = pl.program_id(2)
    slope = sc_ref[h]
    lam = sc_ref[ATT_HEADS]

    first = jnp.logical_and(pl.program_id(0) == 0, jnp.logical_and(h == 0, qi == 0))

    @pl.when(first)
    def _():
        r = lax.broadcasted_iota(jnp.int32, (2 * tq, tk), 0)
        c = lax.broadcasted_iota(jnp.int32, (2 * tq, tk), 1)
        d0_ref[...] = (jnp.where(r >= tq, r - tq, r) - c).astype(F32)

    q = (q_ref[...].astype(F32) * (ATT_HEAD_DIM ** -0.5 * LOG2E)).astype(BF16)
    lane = lax.broadcasted_iota(jnp.int32, (tq, LANES), 1)
    zero = jnp.zeros_like(q)
    qs_ref[0:tq, :] = jnp.where(lane < ATT_HEAD_DIM, q, zero)
    qs_ref[tq:2 * tq, :] = jnp.where(lane >= ATT_HEAD_DIM, q, zero)

    m_ref[...] = jnp.full(m_ref.shape, -1e30, F32)
    l_ref[...] = jnp.zeros(l_ref.shape, F32)
    acc_ref[...] = jnp.zeros(acc_ref.shape, F32)
    q0 = qi * tq

    def body(kj, carry):
        k0 = pl.multiple_of(kj * tk, tk)
        kt = k_ref[pl.ds(k0, tk), :]
        vt = v_ref[pl.ds(k0, tk), :]
        s = lax.dot_general(qs_ref[...], kt, (((1,), (1,)), ((), ())),
                            preferred_element_type=F32)
        delta = (q0 - k0).astype(F32)
        s = s - slope * jnp.abs(d0_ref[...] + delta)
        m_prev = m_ref[...]
        m_new = jnp.maximum(m_prev, jnp.max(s, axis=-1, keepdims=True))
        alpha = jnp.exp2(m_prev - m_new)
        p = jnp.exp2(s - m_new)
        l_ref[...] = alpha * l_ref[...] + jnp.sum(p, axis=-1, keepdims=True)
        acc_ref[...] = alpha * acc_ref[...] + jnp.dot(p.astype(BF16), vt,
                                                      preferred_element_type=F32)
        m_ref[...] = m_new
        return carry

    lax.fori_loop(0, seq // tk, body, 0)

    o = acc_ref[...] / l_ref[...]
    d = o[0:tq, :] - lam * o[tq:2 * tq, :]
    o_ref[...] = (_rms(d, g_ref[...]) * (1.0 - LAMBDA_INIT)).astype(BF16)


def _attention(proj3, scal, g_subln, tq, tk):
    batch, seq, _ = proj3.shape
    kern = functools.partial(_attn_kernel, tq=tq, tk=tk, seq=seq)
    qb, kb, vb = COL_Q * HEAD_BLOCKS, COL_K * HEAD_BLOCKS, COL_VA * HEAD_BLOCKS
    return pl.pallas_call(
        kern,
        grid=(batch, ATT_HEADS, seq // tq),
        in_specs=[
            pl.BlockSpec(memory_space=pltpu.SMEM),
            pl.BlockSpec((None, tq, LANES), lambda b, h, i: (b, i, qb + h)),
            pl.BlockSpec((None, seq, LANES), lambda b, h, i: (b, 0, kb + h)),
            pl.BlockSpec((None, seq, LANES), lambda b, h, i: (b, 0, vb + h)),
            pl.BlockSpec((1, LANES), lambda b, h, i: (0, 0)),
        ],
        out_specs=pl.BlockSpec((None, tq, LANES), lambda b, h, i: (b, i, h)),
        out_shape=jax.ShapeDtypeStruct((batch, seq, D_MODEL), BF16),
        scratch_shapes=[
            pltpu.VMEM((2 * tq, LANES), BF16),
            pltpu.VMEM((2 * tq, tk), F32),
            pltpu.VMEM((2 * tq, 1), F32),
            pltpu.VMEM((2 * tq, 1), F32),
            pltpu.VMEM((2 * tq, LANES), F32),
        ],
        compiler_params=pltpu.CompilerParams(
            dimension_semantics=("arbitrary", "arbitrary", "arbitrary"),
            vmem_limit_bytes=VMEM_LIMIT_BYTES),
        name="diffattn",
    )(scal, proj3, proj3, proj3, g_subln)


def _mix_kernel(u_ref, vn_ref, gs_ref, ga_ref, att_ref, x_ref, ws_ref, bs_ref,
                wbs_ref, wba_ref, wo_ref, gffn_ref, wr_ref, br_ref,
                x1_ref, hn_ref, lg_ref, sgu_ref, *, tm):
    for n in range(tm // CHUNK):
        rows = slice(n * CHUNK, (n + 1) * CHUNK)
        for g in range(SGU_GROUPS):
            cols = slice(g * CHUNK, (g + 1) * CHUNK)
            vm = jnp.dot(ws_ref[g], vn_ref[rows, cols], preferred_element_type=F32) + bs_ref[g]
            sgu_ref[rows, cols] = (u_ref[rows, cols].astype(F32) * vm).astype(BF16)

    a = jnp.dot(sgu_ref[...], wbs_ref[...], preferred_element_type=F32)
    b = jnp.dot(att_ref[...], wba_ref[...], preferred_element_type=F32)
    merged = gs_ref[...].astype(F32) * a + ga_ref[...].astype(F32) * b
    x1 = x_ref[...] + jnp.dot(merged.astype(BF16), wo_ref[...], preferred_element_type=F32)
    x1_ref[...] = x1
    hn = _rms(x1, gffn_ref[...])
    hn_ref[...] = hn
    lg_ref[...] = jnp.dot(hn.astype(BF16), wr_ref[...], preferred_element_type=F32) + br_ref[...]


def _mix(proj, att, x2, ws, bs, wbs, wba, wo, g_ffn, wr, br, tm):
    tokens = x2.shape[0]
    row = lambda c: pl.BlockSpec((tm, D_MODEL), lambda i: (i, c))
    full = lambda shape: pl.BlockSpec(shape, lambda i: (0,) * len(shape))
    return pl.pallas_call(
        functools.partial(_mix_kernel, tm=tm),
        grid=(tokens // tm,),
        in_specs=[
            row(COL_U), row(COL_V), row(COL_GS), row(COL_GA),
            row(0), row(0),
            full((SGU_GROUPS, CHUNK, CHUNK)), full((SGU_GROUPS, CHUNK, CHUNK)),
            full((D_MODEL, D_MODEL)), full((D_MODEL, D_MODEL)), full((D_MODEL, D_MODEL)),
            full((1, D_MODEL)), full((D_MODEL, LANES)), full((1, LANES)),
        ],
        out_specs=[row(0), row(0), pl.BlockSpec((tm, LANES), lambda i: (i, 0))],
        out_shape=[
            jax.ShapeDtypeStruct((tokens, D_MODEL), F32),
            jax.ShapeDtypeStruct((tokens, D_MODEL), F32),
            jax.ShapeDtypeStruct((tokens, LANES), F32),
        ],
        scratch_shapes=[pltpu.VMEM((tm, D_MODEL), BF16)],
        compiler_params=pltpu.CompilerParams(
            dimension_semantics=("arbitrary",),
            vmem_limit_bytes=VMEM_LIMIT_BYTES),
        name="mix",
    )(proj, proj, proj, proj, att, x2, ws, bs, wbs, wba, wo, g_ffn, wr, br)


def _row_copy(src_ref, src_row, dst_ref, dst_row, sem):
    return pltpu.make_async_copy(src_ref.at[pl.ds(src_row, 1)], dst_ref.at[pl.ds(dst_row, 1)], sem)


def _dispatch_kernel(cnt_ref, pend_ref, dest_ref, hn_ref, xs_ref, zero_ref, sem, *, tm, bm):
    @pl.when(pl.program_id(0) == 0)
    def _():
        zero_ref[...] = jnp.zeros(zero_ref.shape, F32)

        def zero_block(start):
            cp = pltpu.make_async_copy(zero_ref, xs_ref.at[pl.ds(pl.multiple_of(start, bm), bm)], sem)
            cp.start()
            cp.wait()

        for e in range(N_EXPERTS):
            @pl.when(cnt_ref[e] > 0)
            def _():
                zero_block(pend_ref[e] - bm)

        def tail(b, carry):
            zero_block(b * bm)
            return carry

        lax.fori_loop(pend_ref[N_EXPERTS - 1] // bm, xs_ref.shape[0] // bm, tail, 0)

    def issue(r, carry):
        for k in range(TOP_K):
            _row_copy(hn_ref, r, xs_ref, dest_ref[0, r * TOP_K + k], sem).start()
        return carry

    lax.fori_loop(0, tm, issue, 0)

    def drain(r, carry):
        for k in range(TOP_K):
            _row_copy(hn_ref, 0, xs_ref, 0, sem).wait()
        return carry

    lax.fori_loop(0, tm, drain, 0)


def _dispatch(hn, dest3, counts, padded_end, n_rows_padded, tm, bm):
    tokens = hn.shape[0]
    grid_spec = pltpu.PrefetchScalarGridSpec(
        num_scalar_prefetch=2,
        grid=(tokens // tm,),
        in_specs=[
            pl.BlockSpec((None, 1, tm * TOP_K), lambda i, *_: (i, 0, 0), memory_space=pltpu.SMEM),
            pl.BlockSpec((tm, D_MODEL), lambda i, *_: (i, 0)),
        ],
        out_specs=pl.BlockSpec(memory_space=pl.ANY),
        scratch_shapes=[pltpu.VMEM((bm, D_MODEL), F32), pltpu.SemaphoreType.DMA],
    )
    return pl.pallas_call(
        functools.partial(_dispatch_kernel, tm=tm, bm=bm),
        grid_spec=grid_spec,
        out_shape=jax.ShapeDtypeStruct((n_rows_padded, D_MODEL), F32),
        compiler_params=pltpu.CompilerParams(
            dimension_semantics=("arbitrary",),
            vmem_limit_bytes=VMEM_LIMIT_BYTES),
        name="dispatch",
    )(counts, padded_end, dest3, hn)


def _expert_kernel(be_ref, bv_ref, xs_ref, wgu_ref, bgu_ref, wd_ref, bd_ref, y_ref, *, bm):
    valid = bv_ref[pl.program_id(0)]

    @pl.when(valid > 0)
    def _():
        row = lax.broadcasted_iota(jnp.int32, (bm, D_MODEL), 0)
        x = jnp.where(row < valid, xs_ref[...], 0.0).astype(BF16)
        gu = jnp.dot(x, wgu_ref[...], preferred_element_type=F32) + bgu_ref[...]
        glu = jnp.minimum(gu[:, :D_FF], SWIGLU_LIMIT)
        lin = jnp.clip(gu[:, D_FF:], -SWIGLU_LIMIT, SWIGLU_LIMIT)
        act = glu * jax.nn.sigmoid(SWIGLU_ALPHA * glu) * (lin + 1.0)
        y_ref[...] = jnp.dot(act.astype(BF16), wd_ref[...], preferred_element_type=F32) + bd_ref[...]

    @pl.when(valid <= 0)
    def _():
        y_ref[...] = jnp.zeros(y_ref.shape, F32)


def _experts(xs, block_expert, block_valid, wgu, bgu, wd, bd, bm):
    n_blocks = xs.shape[0] // bm
    grid_spec = pltpu.PrefetchScalarGridSpec(
        num_scalar_prefetch=2,
        grid=(n_blocks,),
        in_specs=[
            pl.BlockSpec((bm, D_MODEL), lambda b, be, bv: (b, 0)),
            pl.BlockSpec((None, D_MODEL, 2 * D_FF), lambda b, be, bv: (be[b], 0, 0)),
            pl.BlockSpec((None, 1, 2 * D_FF), lambda b, be, bv: (be[b], 0, 0)),
            pl.BlockSpec((None, D_FF, D_MODEL), lambda b, be, bv: (be[b], 0, 0)),
            pl.BlockSpec((None, 1, D_MODEL), lambda b, be, bv: (be[b], 0, 0)),
        ],
        out_specs=pl.BlockSpec((bm, D_MODEL), lambda b, be, bv: (b, 0)),
    )
    return pl.pallas_call(
        functools.partial(_expert_kernel, bm=bm),
        grid_spec=grid_spec,
        out_shape=jax.ShapeDtypeStruct(xs.shape, F32),
        compiler_params=pltpu.CompilerParams(
            dimension_semantics=("arbitrary",),
            vmem_limit_bytes=VMEM_LIMIT_BYTES),
        name="experts",
    )(block_expert, block_valid, xs, wgu, bgu, wd, bd)


def _combine_kernel(dest_ref, gate_ref, x1_ref, gfin_ref, y_ref, o_ref, buf_ref, sem, *, tm):
    def issue(r, carry):
        for k in range(TOP_K):
            _row_copy(y_ref, dest_ref[0, r * TOP_K + k], buf_ref.at[k], r, sem).start()
        return carry

    lax.fori_loop(0, tm, issue, 0)

    def drain(r, carry):
        for k in range(TOP_K):
            _row_copy(y_ref, 0, buf_ref.at[k], 0, sem).wait()
        return carry

    lax.fori_loop(0, tm, drain, 0)

    gates = gate_ref[...]
    x = x1_ref[...]
    for k in range(TOP_K):
        x = x + gates[:, k:k + 1] * buf_ref[k]
    o_ref[...] = _rms(x, gfin_ref[...])


def _combine(yrows, dest3, gates, x1, g_final, tm):
    tokens = x1.shape[0]
    grid_spec = pltpu.PrefetchScalarGridSpec(
        num_scalar_prefetch=0,
        grid=(tokens // tm,),
        in_specs=[
            pl.BlockSpec((None, 1, tm * TOP_K), lambda i: (i, 0, 0), memory_space=pltpu.SMEM),
            pl.BlockSpec((tm, TOP_K), lambda i: (i, 0)),
            pl.BlockSpec((tm, D_MODEL), lambda i: (i, 0)),
            pl.BlockSpec((1, D_MODEL), lambda i: (0, 0)),
            pl.BlockSpec(memory_space=pl.ANY),
        ],
        out_specs=pl.BlockSpec((tm, D_MODEL), lambda i: (i, 0)),
        scratch_shapes=[pltpu.VMEM((TOP_K, tm, D_MODEL), F32), pltpu.SemaphoreType.DMA],
    )
    return pl.pallas_call(
        functools.partial(_combine_kernel, tm=tm),
        grid_spec=grid_spec,
        out_shape=jax.ShapeDtypeStruct((tokens, D_MODEL), F32),
        compiler_params=pltpu.CompilerParams(
            dimension_semantics=("arbitrary",),
            vmem_limit_bytes=VMEM_LIMIT_BYTES),
        name="combine",
    )(dest3, gates, x1, g_final, yrows)


def _route(logits, bm):
    tokens = logits.shape[0]
    n_rows = tokens * TOP_K
    n_blocks = -(-n_rows // bm) + N_EXPERTS
    top_val, top_idx = lax.top_k(logits, TOP_K)
    gates = jax.nn.softmax(top_val, axis=-1)
    top_idx = top_idx.astype(jnp.int32)
    sel = jnp.sum((top_idx[:, :, None] == jnp.arange(N_EXPERTS, dtype=jnp.int32)).astype(jnp.int32),
                  axis=1)
    incl = jnp.cumsum(sel, axis=0)
    rank = jnp.take_along_axis(incl - sel, top_idx, axis=1)
    counts = incl[-1]
    padded_counts = (counts + bm - 1) // bm * bm
    padded_end = jnp.cumsum(padded_counts)
    padded_start = padded_end - padded_counts
    dest = padded_start[top_idx] + rank
    block_start = jnp.arange(n_blocks, dtype=jnp.int32) * bm
    block_expert = jnp.minimum(jnp.searchsorted(padded_end, block_start, side='right'),
                               N_EXPERTS - 1).astype(jnp.int32)
    block_valid = jnp.clip(counts[block_expert] - (block_start - padded_start[block_expert]), 0, bm)
    return (gates, dest.astype(jnp.int32), counts.astype(jnp.int32), padded_end.astype(jnp.int32),
            block_expert, block_valid.astype(jnp.int32), n_blocks)


def _token_mixer(x, p):
    batch, seq, _ = x.shape
    t = _tiles(batch, seq)
    x2 = x.reshape(batch * seq, D_MODEL)
    proj = _inproj(x2, p["g_mix"], p["w_in"], p["g_sgu"], t["tm_in"])
    att = _attention(proj.reshape(batch, seq, IN_COLS), p["attn_scalars"], p["g_subln"],
                     t["tq"], t["tk"])
    return _mix(proj, att.reshape(batch * seq, D_MODEL), x2, p["w_spatial"], p["b_spatial"],
                p["w_branch_sgu"], p["w_branch_attn"], p["w_out"], p["g_ffn"],
                p["w_router"], p["b_router"], t["tm_mix"])


def _moe(hn, logits, x1, p):
    tokens = hn.shape[0]
    tm, bm = min(TOKEN_TILE, tokens), ROUTE_BLOCK
    gates, dest, counts, padded_end, block_expert, block_valid, n_blocks = _route(logits, bm)
    dest3 = dest.reshape(tokens // tm, 1, tm * TOP_K)
    xs = _dispatch(hn, dest3, counts, padded_end, n_blocks * bm, tm, bm)
    yrows = _experts(xs, block_expert, block_valid, p["w_gate_up"], p["b_gate_up"],
                     p["w_down"], p["b_down"], bm)
    return _combine(yrows, dest3, gates, x1, p["g_final"], tm)


def _prepare_params(g_mix, w_in, g_sgu, w_spatial, b_spatial, lambda_q1, lambda_k1, lambda_q2,
                    lambda_k2, g_subln, w_branch_sgu, w_branch_attn, w_out, g_ffn, w_router,
                    b_router, w_gate_up, b_gate_up, w_down, b_down, g_final):
    lam = (jnp.exp(jnp.sum(lambda_q1[0].astype(F32) * lambda_k1[0].astype(F32)))
           - jnp.exp(jnp.sum(lambda_q2[0].astype(F32) * lambda_k2[0].astype(F32)))
           + LAMBDA_INIT)
    slopes = jnp.exp2(-8.0 * jnp.arange(1, ATT_HEADS + 1, dtype=F32) / ATT_HEADS) * LOG2E
    pad = LANES - N_EXPERTS
    return dict(
        g_mix=g_mix[0].reshape(1, D_MODEL),
        w_in=w_in[0].astype(BF16),
        g_sgu=g_sgu[0].reshape(1, D_MODEL),
        w_spatial=w_spatial[0].astype(BF16),
        b_spatial=jnp.broadcast_to(b_spatial[0][:, :, None], (SGU_GROUPS, CHUNK, CHUNK)).astype(F32),
        attn_scalars=jnp.concatenate([slopes, lam.reshape(1)]).astype(F32),
        g_subln=g_subln[0].reshape(1, ATT_VDIM),
        w_branch_sgu=w_branch_sgu[0].astype(BF16),
        w_branch_attn=w_branch_attn[0].astype(BF16),
        w_out=w_out[0].astype(BF16),
        g_ffn=g_ffn[0].reshape(1, D_MODEL),
        w_router=jnp.pad(w_router[0], ((0, 0), (0, pad))).astype(BF16),
        b_router=jnp.pad(b_router[0], (0, pad)).reshape(1, LANES).astype(F32),
        w_gate_up=w_gate_up[0].astype(BF16),
        b_gate_up=b_gate_up[0].reshape(N_EXPERTS, 1, 2 * D_FF),
        w_down=w_down[0].astype(BF16),
        b_down=b_down[0].reshape(N_EXPERTS, 1, D_MODEL),
        g_final=g_final.reshape(1, D_MODEL),
    )


def _encoder(x, p):
    batch, seq, _ = x.shape
    x1, hn, logits = _token_mixer(x, p)
    y = _moe(hn, logits[:, :N_EXPERTS], x1, p)
    return y.reshape(batch, seq, D_MODEL)


def kernel(x_prompt, x_sample, g_mix, w_in, g_sgu, w_spatial, b_spatial, lambda_q1, lambda_k1, lambda_q2, lambda_k2, g_subln, w_branch_sgu, w_branch_attn, w_out, g_ffn, w_router, b_router, w_gate_up, b_gate_up, w_down, b_down, g_final):
    p = _prepare_params(g_mix, w_in, g_sgu, w_spatial, b_spatial, lambda_q1, lambda_k1, lambda_q2,
                        lambda_k2, g_subln, w_branch_sgu, w_branch_attn, w_out, g_ffn, w_router,
                        b_router, w_gate_up, b_gate_up, w_down, b_down, g_final)
    return (_encoder(x_prompt, p), _encoder(x_sample, p))
```

```python
import functools
import math

import jax
import jax.numpy as jnp
from jax import lax
from jax.experimental import pallas as pl
from jax.experimental.pallas import tpu as pltpu

D_MODEL = 1024
SGU_GROUPS = 8
CHUNK = 128
ATT_HEADS = 8
ATT_HEAD_DIM = 64
ATT_VDIM = 2 * ATT_HEAD_DIM
N_EXPERTS = 32
TOP_K = 4
D_FF = D_MODEL
SWIGLU_LIMIT = 7.0
SWIGLU_ALPHA = 1.702
RMS_EPS = 1e-5
LAMBDA_INIT = 0.8 - 0.6 * math.exp(-0.3 * 0)
LOG2E = 1.4426950408889634

COL_U, COL_V, COL_Q, COL_K, COL_VA, COL_GS, COL_GA = range(7)
IN_COLS = 7 * D_MODEL
LANES = 128
HEAD_BLOCKS = D_MODEL // LANES

VMEM_LIMIT_BYTES = 56 * 1024 * 1024

F32 = jnp.float32
BF16 = jnp.bfloat16


def _tiles(batch, seq):
    tokens = batch * seq
    return dict(
        tm_in=min(1024, tokens),
        tq=min(256, seq),
        tk=min(512, seq),
        tm_mix=min(512, tokens),
    )


ROUTE_BLOCK = 256
TOKEN_TILE = 256


def _gelu_tanh(x):
    return 0.5 * x * (1.0 + jnp.tanh(math.sqrt(2.0 / math.pi) * (x + 0.044715 * (x * x * x))))


def _rms(x, g):
    r = x * lax.rsqrt(jnp.mean(x * x, axis=-1, keepdims=True) + RMS_EPS)
    return r * g


def _inproj_kernel(x_ref, gmix_ref, w_ref, gsgu_ref, o_ref, xn_ref):
    j = pl.program_id(1)

    @pl.when(j == 0)
    def _():
        xn_ref[...] = _rms(x_ref[...], gmix_ref[...]).astype(BF16)

    y = jnp.dot(xn_ref[...], w_ref[...], preferred_element_type=F32)

    @pl.when(j == COL_U)
    def _():
        o_ref[...] = _gelu_tanh(y).astype(BF16)

    @pl.when(j == COL_V)
    def _():
        o_ref[...] = _rms(_gelu_tanh(y), gsgu_ref[...]).astype(BF16)

    @pl.when(jnp.logical_and(j >= COL_Q, j <= COL_VA))
    def _():
        o_ref[...] = y.astype(BF16)

    @pl.when(j >= COL_GS)
    def _():
        o_ref[...] = jax.nn.sigmoid(y).astype(BF16)


def _inproj(x2, g_mix, w_in, g_sgu, tm):
    tokens = x2.shape[0]
    return pl.pallas_call(
        _inproj_kernel,
        grid=(tokens // tm, IN_COLS // D_MODEL),
        in_specs=[
            pl.BlockSpec((tm, D_MODEL), lambda i, j: (i, 0)),
            pl.BlockSpec((1, D_MODEL), lambda i, j: (0, 0)),
            pl.BlockSpec((D_MODEL, D_MODEL), lambda i, j: (0, j)),
            pl.BlockSpec((1, D_MODEL), lambda i, j: (0, 0)),
        ],
        out_specs=pl.BlockSpec((tm, D_MODEL), lambda i, j: (i, j)),
        out_shape=jax.ShapeDtypeStruct((tokens, IN_COLS), BF16),
        scratch_shapes=[pltpu.VMEM((tm, D_MODEL), BF16)],
        compiler_params=pltpu.CompilerParams(
            dimension_semantics=("arbitrary", "arbitrary"),
            vmem_limit_bytes=VMEM_LIMIT_BYTES),
        name="inproj",
    )(x2, g_mix, w_in, g_sgu)


def _attn_kernel(sc_ref, q_ref, k_ref, v_ref, g_ref, o_ref,
                 qs_ref, tbl_ref, sb_ref, pm_ref, m_ref, l_ref, acc_ref, *, tq, tk, seq):
    h = pl.program_id(1)
    qi = pl.program_id(2)
    slope = sc_ref[h]
    lam = sc_ref[ATT_HEADS]
    nk = seq // tk
    n_diag = tk // tq
    rows = 2 * tq
    lane_tiles = tk // LANES

    @pl.when(qi == 0)
    def _():
        r = lax.broadcasted_iota(jnp.int32, (rows, tk), 0)
        c = lax.broadcasted_iota(jnp.int32, (rows, tk), 1)
        d0 = (jnp.where(r >= tq, r - tq, r) - c).astype(F32)
        tbl_ref[0] = slope * d0
        tbl_ref[1] = -slope * d0
        for d in range(n_diag):
            tbl_ref[2 + d] = slope * jnp.abs(d0 + float(d * tq))

    q = (q_ref[...].astype(F32) * (ATT_HEAD_DIM ** -0.5 * LOG2E)).astype(BF16)
    lane = lax.broadcasted_iota(jnp.int32, (tq, LANES), 1)
    zero = jnp.zeros_like(q)
    qs_ref[0:tq, :] = jnp.where(lane < ATT_HEAD_DIM, q, zero)
    qs_ref[tq:rows, :] = jnp.where(lane >= ATT_HEAD_DIM, q, zero)

    m_ref[...] = jnp.full(m_ref.shape, -1e30, F32)
    l_ref[...] = jnp.zeros(l_ref.shape, F32)
    acc_ref[...] = jnp.zeros(acc_ref.shape, F32)
    q0 = qi * tq
    kd = q0 // tk

    def tile_consts(kj):
        delta = (q0 - kj * tk).astype(F32)
        left, right = kj < kd, kj > kd
        sel = jnp.where(left, 0, jnp.where(right, 1, 2 + (q0 - kd * tk) // tq))
        cst = jnp.where(left, slope * delta, jnp.where(right, -slope * delta, 0.0))
        return sel, cst

    def scores(kj, slot):
        sel, _ = tile_consts(kj)
        kt = k_ref[pl.ds(pl.multiple_of(kj * tk, tk), tk), :]
        s = lax.dot_general(qs_ref[...], kt, (((1,), (1,)), ((), ())),
                            preferred_element_type=F32)
        pm = None
        for c in range(lane_tiles):
            cols = slice(c * LANES, (c + 1) * LANES)
            sb = s[:, cols] - tbl_ref[sel, :, cols]
            sb_ref[slot, :, cols] = sb
            pm = sb if pm is None else jnp.maximum(pm, sb)
        pm_ref[slot] = pm

    def accumulate(kj, slot):
        _, cst = tile_consts(kj)
        red = jnp.max(pm_ref[slot], axis=-1, keepdims=True) - cst
        m_old = m_ref[...]
        m_new = jnp.maximum(m_old, jnp.broadcast_to(red, m_old.shape))
        alpha = jnp.exp2(m_old - m_new)
        m_ref[...] = m_new
        shift = m_new + cst
        ps, lsum = [], None
        for c in range(lane_tiles):
            p = jnp.exp2(sb_ref[slot, :, c * LANES:(c + 1) * LANES] - shift)
            lsum = p if lsum is None else lsum + p
            ps.append(p.astype(BF16))
        l_ref[...] = alpha * l_ref[...] + lsum
        vt = v_ref[pl.ds(pl.multiple_of(kj * tk, tk), tk), :]
        pv = jnp.dot(jnp.concatenate(ps, axis=1), vt, preferred_element_type=F32)
        acc_ref[...] = alpha * acc_ref[...] + pv

    scores(0, 0)
    for kj in range(nk):
        if kj + 1 < nk:
            scores(kj + 1, (kj + 1) % 2)
        accumulate(kj, kj % 2)

    o = acc_ref[...] / jnp.sum(l_ref[...], axis=-1, keepdims=True)
    d = o[0:tq, :] - lam * o[tq:rows, :]
    o_ref[...] = (_rms(d, g_ref[...]) * (1.0 - LAMBDA_INIT)).astype(BF16)


def _attention(proj3, scal, g_subln, tq, tk):
    batch, seq, _ = proj3.shape
    kern = functools.partial(_attn_kernel, tq=tq, tk=tk, seq=seq)
    qb, kb, vb = COL_Q * HEAD_BLOCKS, COL_K * HEAD_BLOCKS, COL_VA * HEAD_BLOCKS
    return pl.pallas_call(
        kern,
        grid=(batch, ATT_HEADS, seq // tq),
        in_specs=[
            pl.BlockSpec(memory_space=pltpu.SMEM),
            pl.BlockSpec((None, tq, LANES), lambda b, h, i: (b, i, qb + h)),
            pl.BlockSpec((None, seq, LANES), lambda b, h, i: (b, 0, kb + h)),
            pl.BlockSpec((None, seq, LANES), lambda b, h, i: (b, 0, vb + h)),
            pl.BlockSpec((1, LANES), lambda b, h, i: (0, 0)),
        ],
        out_specs=pl.BlockSpec((None, tq, LANES), lambda b, h, i: (b, i, h)),
        out_shape=jax.ShapeDtypeStruct((batch, seq, D_MODEL), BF16),
        scratch_shapes=[
            pltpu.VMEM((2 * tq, LANES), BF16),
            pltpu.VMEM((2 + tk // tq, 2 * tq, tk), F32),
            pltpu.VMEM((2, 2 * tq, tk), F32),
            pltpu.VMEM((2, 2 * tq, LANES), F32),
            pltpu.VMEM((2 * tq, LANES), F32),
            pltpu.VMEM((2 * tq, LANES), F32),
            pltpu.VMEM((2 * tq, LANES), F32),
        ],
        compiler_params=pltpu.CompilerParams(
            dimension_semantics=("arbitrary", "arbitrary", "arbitrary"),
            vmem_limit_bytes=VMEM_LIMIT_BYTES),
        name="diffattn",
    )(scal, proj3, proj3, proj3, g_subln)


def _mix_kernel(u_ref, vn_ref, gs_ref, ga_ref, att_ref, x_ref, ws_ref, bs_ref,
                wbs_ref, wba_ref, wo_ref, gffn_ref, wr_ref, br_ref,
                x1_ref, hn_ref, lg_ref, sgu_ref, *, tm):
    for n in range(tm // CHUNK):
        rows = slice(n * CHUNK, (n + 1) * CHUNK)
        for g in range(SGU_GROUPS):
            cols = slice(g * CHUNK, (g + 1) * CHUNK)
            vm = jnp.dot(ws_ref[g], vn_ref[rows, cols], preferred_element_type=F32) + bs_ref[g]
            sgu_ref[rows, cols] = (u_ref[rows, cols].astype(F32) * vm).astype(BF16)

    a = jnp.dot(sgu_ref[...], wbs_ref[...], preferred_element_type=F32)
    b = jnp.dot(att_ref[...], wba_ref[...], preferred_element_type=F32)
    merged = gs_ref[...].astype(F32) * a + ga_ref[...].astype(F32) * b
    x1 = x_ref[...] + jnp.dot(merged.astype(BF16), wo_ref[...], preferred_element_type=F32)
    x1_ref[...] = x1
    hn = _rms(x1, gffn_ref[...])
    hn_ref[...] = hn
    lg_ref[...] = jnp.dot(hn.astype(BF16), wr_ref[...], preferred_element_type=F32) + br_ref[...]


def _mix(proj, att, x2, ws, bs, wbs, wba, wo, g_ffn, wr, br, tm):
    tokens = x2.shape[0]
    row = lambda c: pl.BlockSpec((tm, D_MODEL), lambda i: (i, c))
    full = lambda shape: pl.BlockSpec(shape, lambda i: (0,) * len(shape))
    return pl.pallas_call(
        functools.partial(_mix_kernel, tm=tm),
        grid=(tokens // tm,),
        in_specs=[
            row(COL_U), row(COL_V), row(COL_GS), row(COL_GA),
            row(0), row(0),
            full((SGU_GROUPS, CHUNK, CHUNK)), full((SGU_GROUPS, CHUNK, CHUNK)),
            full((D_MODEL, D_MODEL)), full((D_MODEL, D_MODEL)), full((D_MODEL, D_MODEL)),
            full((1, D_MODEL)), full((D_MODEL, LANES)), full((1, LANES)),
        ],
        out_specs=[row(0), row(0), pl.BlockSpec((tm, LANES), lambda i: (i, 0))],
        out_shape=[
            jax.ShapeDtypeStruct((tokens, D_MODEL), F32),
            jax.ShapeDtypeStruct((tokens, D_MODEL), F32),
            jax.ShapeDtypeStruct((tokens, LANES), F32),
        ],
        scratch_shapes=[pltpu.VMEM((tm, D_MODEL), BF16)],
        compiler_params=pltpu.CompilerParams(
            dimension_semantics=("arbitrary",),
            vmem_limit_bytes=VMEM_LIMIT_BYTES),
        name="mix",
    )(proj, proj, proj, proj, att, x2, ws, bs, wbs, wba, wo, g_ffn, wr, br)


def _row_copy(src_ref, src_row, dst_ref, dst_row, sem):
    return pltpu.make_async_copy(src_ref.at[pl.ds(src_row, 1)], dst_ref.at[pl.ds(dst_row, 1)], sem)


def _dispatch_kernel(cnt_ref, pend_ref, dest_ref, hn_ref, xs_ref, zero_ref, sem, *, tm, bm):
    @pl.when(pl.program_id(0) == 0)
    def _():
        zero_ref[...] = jnp.zeros(zero_ref.shape, F32)

        def zero_block(start):
            cp = pltpu.make_async_copy(zero_ref, xs_ref.at[pl.ds(pl.multiple_of(start, bm), bm)], sem)
            cp.start()
            cp.wait()

        for e in range(N_EXPERTS):
            @pl.when(cnt_ref[e] > 0)
            def _():
                zero_block(pend_ref[e] - bm)

        def tail(b, carry):
            zero_block(b * bm)
            return carry

        lax.fori_loop(pend_ref[N_EXPERTS - 1] // bm, xs_ref.shape[0] // bm, tail, 0)

    def issue(r, carry):
        for k in range(TOP_K):
            _row_copy(hn_ref, r, xs_ref, dest_ref[0, r * TOP_K + k], sem).start()
        return carry

    lax.fori_loop(0, tm, issue, 0)

    def drain(r, carry):
        for k in range(TOP_K):
            _row_copy(hn_ref, 0, xs_ref, 0, sem).wait()
        return carry

    lax.fori_loop(0, tm, drain, 0)


def _dispatch(hn, dest3, counts, padded_end, n_rows_padded, tm, bm):
    tokens = hn.shape[0]
    grid_spec = pltpu.PrefetchScalarGridSpec(
        num_scalar_prefetch=2,
        grid=(tokens // tm,),
        in_specs=[
            pl.BlockSpec((None, 1, tm * TOP_K), lambda i, *_: (i, 0, 0), memory_space=pltpu.SMEM),
            pl.BlockSpec((tm, D_MODEL), lambda i, *_: (i, 0)),
        ],
        out_specs=pl.BlockSpec(memory_space=pl.ANY),
        scratch_shapes=[pltpu.VMEM((bm, D_MODEL), F32), pltpu.SemaphoreType.DMA],
    )
    return pl.pallas_call(
        functools.partial(_dispatch_kernel, tm=tm, bm=bm),
        grid_spec=grid_spec,
        out_shape=jax.ShapeDtypeStruct((n_rows_padded, D_MODEL), F32),
        compiler_params=pltpu.CompilerParams(
            dimension_semantics=("arbitrary",),
            vmem_limit_bytes=VMEM_LIMIT_BYTES),
        name="dispatch",
    )(counts, padded_end, dest3, hn)


def _expert_kernel(be_ref, bv_ref, xs_ref, wgu_ref, bgu_ref, wd_ref, bd_ref, y_ref, *, bm):
    valid = bv_ref[pl.program_id(0)]

    @pl.when(valid > 0)
    def _():
        row = lax.broadcasted_iota(jnp.int32, (bm, D_MODEL), 0)
        x = jnp.where(row < valid, xs_ref[...], 0.0).astype(BF16)
        gu = jnp.dot(x, wgu_ref[...], preferred_element_type=F32) + bgu_ref[...]
        glu = jnp.minimum(gu[:, :D_FF], SWIGLU_LIMIT)
        lin = jnp.clip(gu[:, D_FF:], -SWIGLU_LIMIT, SWIGLU_LIMIT)
        act = glu * jax.nn.sigmoid(SWIGLU_ALPHA * glu) * (lin + 1.0)
        y_ref[...] = jnp.dot(act.astype(BF16), wd_ref[...], preferred_element_type=F32) + bd_ref[...]

    @pl.when(valid <= 0)
    def _():
        y_ref[...] = jnp.zeros(y_ref.shape, F32)


def _experts(xs, block_expert, block_valid, wgu, bgu, wd, bd, bm):
    n_blocks = xs.shape[0] // bm
    grid_spec = pltpu.PrefetchScalarGridSpec(
        num_scalar_prefetch=2,
        grid=(n_blocks,),
        in_specs=[
            pl.BlockSpec((bm, D_MODEL), lambda b, be, bv: (b, 0)),
            pl.BlockSpec((None, D_MODEL, 2 * D_FF), lambda b, be, bv: (be[b], 0, 0)),
            pl.BlockSpec((None, 1, 2 * D_FF), lambda b, be, bv: (be[b], 0, 0)),
            pl.BlockSpec((None, D_FF, D_MODEL), lambda b, be, bv: (be[b], 0, 0)),
            pl.BlockSpec((None, 1, D_MODEL), lambda b, be, bv: (be[b], 0, 0)),
        ],
        out_specs=pl.BlockSpec((bm, D_MODEL), lambda b, be, bv: (b, 0)),
    )
    return pl.pallas_call(
        functools.partial(_expert_kernel, bm=bm),
        grid_spec=grid_spec,
        out_shape=jax.ShapeDtypeStruct(xs.shape, F32),
        compiler_params=pltpu.CompilerParams(
            dimension_semantics=("arbitrary",),
            vmem_limit_bytes=VMEM_LIMIT_BYTES),
        name="experts",
    )(block_expert, block_valid, xs, wgu, bgu, wd, bd)


def _combine_kernel(dest_ref, gate_ref, x1_ref, gfin_ref, y_ref, o_ref, buf_ref, sem, *, tm):
    def issue(r, carry):
        for k in range(TOP_K):
            _row_copy(y_ref, dest_ref[0, r * TOP_K + k], buf_ref.at[k], r, sem).start()
        return carry

    lax.fori_loop(0, tm, issue, 0)

    def drain(r, carry):
        for k in range(TOP_K):
            _row_copy(y_ref, 0, buf_ref.at[k], 0, sem).wait()
        return carry

    lax.fori_loop(0, tm, drain, 0)

    gates = gate_ref[...]
    x = x1_ref[...]
    for k in range(TOP_K):
        x = x + gates[:, k:k + 1] * buf_ref[k]
    o_ref[...] = _rms(x, gfin_ref[...])


def _combine(yrows, dest3, gates, x1, g_final, tm):
    tokens = x1.shape[0]
    grid_spec = pltpu.PrefetchScalarGridSpec(
        num_scalar_prefetch=0,
        grid=(tokens // tm,),
        in_specs=[
            pl.BlockSpec((None, 1, tm * TOP_K), lambda i: (i, 0, 0), memory_space=pltpu.SMEM),
            pl.BlockSpec((tm, TOP_K), lambda i: (i, 0)),
            pl.BlockSpec((tm, D_MODEL), lambda i: (i, 0)),
            pl.BlockSpec((1, D_MODEL), lambda i: (0, 0)),
            pl.BlockSpec(memory_space=pl.ANY),
        ],
        out_specs=pl.BlockSpec((tm, D_MODEL), lambda i: (i, 0)),
        scratch_shapes=[pltpu.VMEM((TOP_K, tm, D_MODEL), F32), pltpu.SemaphoreType.DMA],
    )
    return pl.pallas_call(
        functools.partial(_combine_kernel, tm=tm),
        grid_spec=grid_spec,
        out_shape=jax.ShapeDtypeStruct((tokens, D_MODEL), F32),
        compiler_params=pltpu.CompilerParams(
            dimension_semantics=("arbitrary",),
            vmem_limit_bytes=VMEM_LIMIT_BYTES),
        name="combine",
    )(dest3, gates, x1, g_final, yrows)


def _route(logits, bm):
    tokens = logits.shape[0]
    n_rows = tokens * TOP_K
    n_blocks = -(-n_rows // bm) + N_EXPERTS
    top_val, top_idx = lax.top_k(logits, TOP_K)
    gates = jax.nn.softmax(top_val, axis=-1)
    top_idx = top_idx.astype(jnp.int32)
    sel = jnp.sum((top_idx[:, :, None] == jnp.arange(N_EXPERTS, dtype=jnp.int32)).astype(jnp.int32),
                  axis=1)
    incl = jnp.cumsum(sel, axis=0)
    rank = jnp.take_along_axis(incl - sel, top_idx, axis=1)
    counts = incl[-1]
    padded_counts = (counts + bm - 1) // bm * bm
    padded_end = jnp.cumsum(padded_counts)
    padded_start = padded_end - padded_counts
    dest = padded_start[top_idx] + rank
    block_start = jnp.arange(n_blocks, dtype=jnp.int32) * bm
    block_expert = jnp.minimum(jnp.searchsorted(padded_end, block_start, side='right'),
                               N_EXPERTS - 1).astype(jnp.int32)
    block_valid = jnp.clip(counts[block_expert] - (block_start - padded_start[block_expert]), 0, bm)
    return (gates, dest.astype(jnp.int32), counts.astype(jnp.int32), padded_end.astype(jnp.int32),
            block_expert, block_valid.astype(jnp.int32), n_blocks)


def _token_mixer(x, p):
    batch, seq, _ = x.shape
    t = _tiles(batch, seq)
    x2 = x.reshape(batch * seq, D_MODEL)
    proj = _inproj(x2, p["g_mix"], p["w_in"], p["g_sgu"], t["tm_in"])
    att = _attention(proj.reshape(batch, seq, IN_COLS), p["attn_scalars"], p["g_subln"],
                     t["tq"], t["tk"])
    return _mix(proj, att.reshape(batch * seq, D_MODEL), x2, p["w_spatial"], p["b_spatial"],
                p["w_branch_sgu"], p["w_branch_attn"], p["w_out"], p["g_ffn"],
                p["w_router"], p["b_router"], t["tm_mix"])


def _moe(hn, logits, x1, p):
    tokens = hn.shape[0]
    tm, bm = min(TOKEN_TILE, tokens), ROUTE_BLOCK
    gates, dest, counts, padded_end, block_expert, block_valid, n_blocks = _route(logits, bm)
    dest3 = dest.reshape(tokens // tm, 1, tm * TOP_K)
    xs = _dispatch(hn, dest3, counts, padded_end, n_blocks * bm, tm, bm)
    yrows = _experts(xs, block_expert, block_valid, p["w_gate_up"], p["b_gate_up"],
                     p["w_down"], p["b_down"], bm)
    return _combine(yrows, dest3, gates, x1, p["g_final"], tm)


def _prepare_params(g_mix, w_in, g_sgu, w_spatial, b_spatial, lambda_q1, lambda_k1, lambda_q2,
                    lambda_k2, g_subln, w_branch_sgu, w_branch_attn, w_out, g_ffn, w_router,
                    b_router, w_gate_up, b_gate_up, w_down, b_down, g_final):
    lam = (jnp.exp(jnp.sum(lambda_q1[0].astype(F32) * lambda_k1[0].astype(F32)))
           - jnp.exp(jnp.sum(lambda_q2[0].astype(F32) * lambda_k2[0].astype(F32)))
           + LAMBDA_INIT)
    slopes = jnp.exp2(-8.0 * jnp.arange(1, ATT_HEADS + 1, dtype=F32) / ATT_HEADS) * LOG2E
    pad = LANES - N_EXPERTS
    return dict(
        g_mix=g_mix[0].reshape(1, D_MODEL),
        w_in=w_in[0].astype(BF16),
        g_sgu=g_sgu[0].reshape(1, D_MODEL),
        w_spatial=w_spatial[0].astype(BF16),
        b_spatial=jnp.broadcast_to(b_spatial[0][:, :, None], (SGU_GROUPS, CHUNK, CHUNK)).astype(F32),
        attn_scalars=jnp.concatenate([slopes, lam.reshape(1)]).astype(F32),
        g_subln=g_subln[0].reshape(1, ATT_VDIM),
        w_branch_sgu=w_branch_sgu[0].astype(BF16),
        w_branch_attn=w_branch_attn[0].astype(BF16),
        w_out=w_out[0].astype(BF16),
        g_ffn=g_ffn[0].reshape(1, D_MODEL),
        w_router=jnp.pad(w_router[0], ((0, 0), (0, pad))).astype(BF16),
        b_router=jnp.pad(b_router[0], (0, pad)).reshape(1, LANES).astype(F32),
        w_gate_up=w_gate_up[0].astype(BF16),
        b_gate_up=b_gate_up[0].reshape(N_EXPERTS, 1, 2 * D_FF),
        w_down=w_down[0].astype(BF16),
        b_down=b_down[0].reshape(N_EXPERTS, 1, D_MODEL),
        g_final=g_final.reshape(1, D_MODEL),
    )


def _encoder(x, p):
    batch, seq, _ = x.shape
    x1, hn, logits = _token_mixer(x, p)
    y = _moe(hn, logits[:, :N_EXPERTS], x1, p)
    return y.reshape(batch, seq, D_MODEL)


def kernel(x_prompt, x_sample, g_mix, w_in, g_sgu, w_spatial, b_spatial, lambda_q1, lambda_k1, lambda_q2, lambda_k2, g_subln, w_branch_sgu, w_branch_attn, w_out, g_ffn, w_router, b_router, w_gate_up, b_gate_up, w_down, b_down, g_final):
    p = _prepare_params(g_mix, w_in, g_sgu, w_spatial, b_spatial, lambda_q1, lambda_k1, lambda_q2,
                        lambda_k2, g_subln, w_branch_sgu, w_branch_attn, w_out, g_ffn, w_router,
                        b_router, w_gate_up, b_gate_up, w_down, b_down, g_final)
    return (_encoder(x_prompt, p), _encoder(x_sample, p))
```

```python
import functools
import math

import jax
import jax.numpy as jnp
from jax import lax
from jax.experimental import pallas as pl
from jax.experimental.pallas import tpu as pltpu

D_MODEL = 1024
SGU_GROUPS = 8
CHUNK = 128
ATT_HEADS = 8
ATT_HEAD_DIM = 64
ATT_VDIM = 2 * ATT_HEAD_DIM
N_EXPERTS = 32
TOP_K = 4
D_FF = D_MODEL
SWIGLU_LIMIT = 7.0
SWIGLU_ALPHA = 1.702
RMS_EPS = 1e-5
LAMBDA_INIT = 0.8 - 0.6 * math.exp(-0.3 * 0)
LOG2E = 1.4426950408889634

COL_U, COL_V, COL_Q, COL_K, COL_VA, COL_GS, COL_GA = range(7)
IN_COLS = 7 * D_MODEL
LANES = 128
HEAD_BLOCKS = D_MODEL // LANES

VMEM_LIMIT_BYTES = 56 * 1024 * 1024

F32 = jnp.float32
BF16 = jnp.bfloat16
U32 = jnp.uint32
I32 = jnp.int32


def _tiles(batch, seq):
    tokens = batch * seq
    return dict(
        tm_in=min(1024, tokens),
        tq=min(256, seq),
        tk=min(512, seq),
        tm_mix=min(512, tokens),
    )


ROUTE_BLOCK = 256
TOKEN_TILE = 256
RUN_CHUNK = 16
ROW_ALIGN = 8
SORTED_ROWS = -(-(TOKEN_TILE * TOP_K + N_EXPERTS * (RUN_CHUNK - 1)) // LANES) * LANES
MAX_COPIES = SORTED_ROWS // RUN_CHUNK
META_WIDTH = -(-(1 + 2 * MAX_COPIES) // LANES) * LANES
PACKED = D_MODEL // 2
HI_MASK = 0xFFFF0000


def _gelu_tanh(x):
    return 0.5 * x * (1.0 + jnp.tanh(math.sqrt(2.0 / math.pi) * (x + 0.044715 * (x * x * x))))


def _rms(x, g):
    r = x * lax.rsqrt(jnp.mean(x * x, axis=-1, keepdims=True) + RMS_EPS)
    return r * g


def _pack_rows(x):
    lo = lax.bitcast_convert_type(x[:, :PACKED], U32) >> jnp.uint32(16)
    hi = lax.bitcast_convert_type(x[:, PACKED:], U32) & jnp.uint32(HI_MASK)
    return lo | hi


def _unpack_rows(w):
    lo = lax.bitcast_convert_type(w << jnp.uint32(16), F32).astype(BF16)
    hi = lax.bitcast_convert_type(w & jnp.uint32(HI_MASK), F32).astype(BF16)
    return lo, hi


def _inproj_kernel(x_ref, gmix_ref, w_ref, gsgu_ref, o_ref, xn_ref):
    j = pl.program_id(1)

    @pl.when(j == 0)
    def _():
        xn_ref[...] = _rms(x_ref[...], gmix_ref[...]).astype(BF16)

    y = jnp.dot(xn_ref[...], w_ref[...], preferred_element_type=F32)

    @pl.when(j == COL_U)
    def _():
        o_ref[...] = _gelu_tanh(y).astype(BF16)

    @pl.when(j == COL_V)
    def _():
        o_ref[...] = _rms(_gelu_tanh(y), gsgu_ref[...]).astype(BF16)

    @pl.when(jnp.logical_and(j >= COL_Q, j <= COL_VA))
    def _():
        o_ref[...] = y.astype(BF16)

    @pl.when(j >= COL_GS)
    def _():
        o_ref[...] = jax.nn.sigmoid(y).astype(BF16)


def _inproj(x2, g_mix, w_in, g_sgu, tm):
    tokens = x2.shape[0]
    return pl.pallas_call(
        _inproj_kernel,
        grid=(tokens // tm, IN_COLS // D_MODEL),
        in_specs=[
            pl.BlockSpec((tm, D_MODEL), lambda i, j: (i, 0)),
            pl.BlockSpec((1, D_MODEL), lambda i, j: (0, 0)),
            pl.BlockSpec((D_MODEL, D_MODEL), lambda i, j: (0, j)),
            pl.BlockSpec((1, D_MODEL), lambda i, j: (0, 0)),
        ],
        out_specs=pl.BlockSpec((tm, D_MODEL), lambda i, j: (i, j)),
        out_shape=jax.ShapeDtypeStruct((tokens, IN_COLS), BF16),
        scratch_shapes=[pltpu.VMEM((tm, D_MODEL), BF16)],
        compiler_params=pltpu.CompilerParams(
            dimension_semantics=("arbitrary", "arbitrary"),
            vmem_limit_bytes=VMEM_LIMIT_BYTES),
        name="inproj",
    )(x2, g_mix, w_in, g_sgu)


def _attn_kernel(sc_ref, q_ref, k_ref, v_ref, g_ref, o_ref,
                 qs_ref, tbl_ref, sb_ref, pm_ref, m_ref, l_ref, acc_ref, *, tq, tk, seq):
    h = pl.program_id(1)
    qi = pl.program_id(2)
    slope = sc_ref[h]
    lam = sc_ref[ATT_HEADS]
    nk = seq // tk
    n_diag = tk // tq
    rows = 2 * tq
    lane_tiles = tk // LANES

    @pl.when(qi == 0)
    def _():
        r = lax.broadcasted_iota(jnp.int32, (rows, tk), 0)
        c = lax.broadcasted_iota(jnp.int32, (rows, tk), 1)
        d0 = (jnp.where(r >= tq, r - tq, r) - c).astype(F32)
        tbl_ref[0] = slope * d0
        tbl_ref[1] = -slope * d0
        for d in range(n_diag):
            tbl_ref[2 + d] = slope * jnp.abs(d0 + float(d * tq))

    q = (q_ref[...].astype(F32) * (ATT_HEAD_DIM ** -0.5 * LOG2E)).astype(BF16)
    lane = lax.broadcasted_iota(jnp.int32, (tq, LANES), 1)
    zero = jnp.zeros_like(q)
    qs_ref[0:tq, :] = jnp.where(lane < ATT_HEAD_DIM, q, zero)
    qs_ref[tq:rows, :] = jnp.where(lane >= ATT_HEAD_DIM, q, zero)

    m_ref[...] = jnp.full(m_ref.shape, -1e30, F32)
    l_ref[...] = jnp.zeros(l_ref.shape, F32)
    acc_ref[...] = jnp.zeros(acc_ref.shape, F32)
    q0 = qi * tq
    kd = q0 // tk

    def tile_consts(kj):
        delta = (q0 - kj * tk).astype(F32)
        left, right = kj < kd, kj > kd
        sel = jnp.where(left, 0, jnp.where(right, 1, 2 + (q0 - kd * tk) // tq))
        cst = jnp.where(left, slope * delta, jnp.where(right, -slope * delta, 0.0))
        return sel, cst

    def scores(kj, slot):
        sel, _ = tile_consts(kj)
        kt = k_ref[pl.ds(pl.multiple_of(kj * tk, tk), tk), :]
        s = lax.dot_general(qs_ref[...], kt, (((1,), (1,)), ((), ())),
                            preferred_element_type=F32)
        pm = None
        for c in range(lane_tiles):
            cols = slice(c * LANES, (c + 1) * LANES)
            sb = s[:, cols] - tbl_ref[sel, :, cols]
            sb_ref[slot, :, cols] = sb
            pm = sb if pm is None else jnp.maximum(pm, sb)
        pm_ref[slot] = pm

    def accumulate(kj, slot):
        _, cst = tile_consts(kj)
        red = jnp.max(pm_ref[slot], axis=-1, keepdims=True) - cst
        m_old = m_ref[...]
        m_new = jnp.maximum(m_old, jnp.broadcast_to(red, m_old.shape))
        alpha = jnp.exp2(m_old - m_new)
        m_ref[...] = m_new
        shift = m_new + cst
        ps, lsum = [], None
        for c in range(lane_tiles):
            p = jnp.exp2(sb_ref[slot, :, c * LANES:(c + 1) * LANES] - shift)
            lsum = p if lsum is None else lsum + p
            ps.append(p.astype(BF16))
        l_ref[...] = alpha * l_ref[...] + lsum
        vt = v_ref[pl.ds(pl.multiple_of(kj * tk, tk), tk), :]
        pv = jnp.dot(jnp.concatenate(ps, axis=1), vt, preferred_element_type=F32)
        acc_ref[...] = alpha * acc_ref[...] + pv

    scores(0, 0)
    for kj in range(nk):
        if kj + 1 < nk:
            scores(kj + 1, (kj + 1) % 2)
        accumulate(kj, kj % 2)

    o = acc_ref[...] / jnp.sum(l_ref[...], axis=-1, keepdims=True)
    d = o[0:tq, :] - lam * o[tq:rows, :]
    o_ref[...] = (_rms(d, g_ref[...]) * (1.0 - LAMBDA_INIT)).astype(BF16)


def _attention(proj3, scal, g_subln, tq, tk):
    batch, seq, _ = proj3.shape
    kern = functools.partial(_attn_kernel, tq=tq, tk=tk, seq=seq)
    qb, kb, vb = COL_Q * HEAD_BLOCKS, COL_K * HEAD_BLOCKS, COL_VA * HEAD_BLOCKS
    return pl.pallas_call(
        kern,
        grid=(batch, ATT_HEADS, seq // tq),
        in_specs=[
            pl.BlockSpec(memory_space=pltpu.SMEM),
            pl.BlockSpec((None, tq, LANES), lambda b, h, i: (b, i, qb + h)),
            pl.BlockSpec((None, seq, LANES), lambda b, h, i: (b, 0, kb + h)),
            pl.BlockSpec((None, seq, LANES), lambda b, h, i: (b, 0, vb + h)),
            pl.BlockSpec((1, LANES), lambda b, h, i: (0, 0)),
        ],
        out_specs=pl.BlockSpec((None, tq, LANES), lambda b, h, i: (b, i, h)),
        out_shape=jax.ShapeDtypeStruct((batch, seq, D_MODEL), BF16),
        scratch_shapes=[
            pltpu.VMEM((2 * tq, LANES), BF16),
            pltpu.VMEM((2 + tk // tq, 2 * tq, tk), F32),
            pltpu.VMEM((2, 2 * tq, tk), F32),
            pltpu.VMEM((2, 2 * tq, LANES), F32),
            pltpu.VMEM((2 * tq, LANES), F32),
            pltpu.VMEM((2 * tq, LANES), F32),
            pltpu.VMEM((2 * tq, LANES), F32),
        ],
        compiler_params=pltpu.CompilerParams(
            dimension_semantics=("arbitrary", "arbitrary", "arbitrary"),
            vmem_limit_bytes=VMEM_LIMIT_BYTES),
        name="diffattn",
    )(scal, proj3, proj3, proj3, g_subln)


def _mix_kernel(u_ref, vn_ref, gs_ref, ga_ref, att_ref, x_ref, ws_ref, bs_ref,
                wbs_ref, wba_ref, wo_ref, gffn_ref, wr_ref, br_ref,
                x1_ref, hn_ref, lg_ref, sgu_ref, *, tm):
    for n in range(tm // CHUNK):
        rows = slice(n * CHUNK, (n + 1) * CHUNK)
        for g in range(SGU_GROUPS):
            cols = slice(g * CHUNK, (g + 1) * CHUNK)
            vm = jnp.dot(ws_ref[g], vn_ref[rows, cols], preferred_element_type=F32) + bs_ref[g]
            sgu_ref[rows, cols] = (u_ref[rows, cols].astype(F32) * vm).astype(BF16)

    a = jnp.dot(sgu_ref[...], wbs_ref[...], preferred_element_type=F32)
    b = jnp.dot(att_ref[...], wba_ref[...], preferred_element_type=F32)
    merged = gs_ref[...].astype(F32) * a + ga_ref[...].astype(F32) * b
    x1 = x_ref[...] + jnp.dot(merged.astype(BF16), wo_ref[...], preferred_element_type=F32)
    x1_ref[...] = x1
    hn = _rms(x1, gffn_ref[...]).astype(BF16)
    hn_ref[...] = hn
    lg_ref[...] = jnp.dot(hn, wr_ref[...], preferred_element_type=F32) + br_ref[...]


def _mix(proj, att, x2, ws, bs, wbs, wba, wo, g_ffn, wr, br, tm):
    tokens = x2.shape[0]
    row = lambda c: pl.BlockSpec((tm, D_MODEL), lambda i: (i, c))
    full = lambda shape: pl.BlockSpec(shape, lambda i: (0,) * len(shape))
    return pl.pallas_call(
        functools.partial(_mix_kernel, tm=tm),
        grid=(tokens // tm,),
        in_specs=[
            row(COL_U), row(COL_V), row(COL_GS), row(COL_GA),
            row(0), row(0),
            full((SGU_GROUPS, CHUNK, CHUNK)), full((SGU_GROUPS, CHUNK, CHUNK)),
            full((D_MODEL, D_MODEL)), full((D_MODEL, D_MODEL)), full((D_MODEL, D_MODEL)),
            full((1, D_MODEL)), full((D_MODEL, LANES)), full((1, LANES)),
        ],
        out_specs=[row(0), row(0), pl.BlockSpec((tm, LANES), lambda i: (i, 0))],
        out_shape=[
            jax.ShapeDtypeStruct((tokens, D_MODEL), F32),
            jax.ShapeDtypeStruct((tokens, D_MODEL), BF16),
            jax.ShapeDtypeStruct((tokens, LANES), F32),
        ],
        scratch_shapes=[pltpu.VMEM((tm, D_MODEL), BF16)],
        compiler_params=pltpu.CompilerParams(
            dimension_semantics=("arbitrary",),
            vmem_limit_bytes=VMEM_LIMIT_BYTES),
        name="mix",
    )(proj, proj, proj, proj, att, x2, ws, bs, wbs, wba, wo, g_ffn, wr, br)


def _run_copy(src_ref, src_row, dst_ref, dst_row, sem):
    return pltpu.make_async_copy(src_ref.at[pl.ds(src_row, RUN_CHUNK)],
                                 dst_ref.at[pl.ds(dst_row, RUN_CHUNK)], sem)


def _dispatch_kernel(bv_ref, meta_ref, post_ref, *rest, bm, tile_starts):
    hn_refs = rest[:len(tile_starts)]
    xs_ref, srt_ref, zero_ref, sem = rest[len(tile_starts):]
    i = pl.program_id(0)

    @pl.when(i == 0)
    def _():
        zero_ref[...] = jnp.zeros(zero_ref.shape, U32)

        def maybe_zero(b, carry):
            @pl.when(bv_ref[b] < bm)
            def _():
                cp = pltpu.make_async_copy(
                    zero_ref, xs_ref.at[pl.ds(pl.multiple_of(b * bm, bm), bm)], sem)
                cp.start()
                cp.wait()
            return carry

        lax.fori_loop(0, xs_ref.shape[0] // bm, maybe_zero, 0)

    hn = hn_refs[0][...]
    for ref, start in zip(hn_refs[1:], tile_starts[1:]):
        hn = jnp.where(i >= start, ref[...], hn)

    slot = lax.broadcasted_iota(I32, (SORTED_ROWS, TOKEN_TILE), 0)
    hit = slot == post_ref[0:1, :]
    for k in range(1, TOP_K):
        hit = jnp.logical_or(hit, slot == post_ref[k:k + 1, :])
    onehot = jnp.where(hit, 1.0, 0.0).astype(BF16)
    srt_ref[...] = _pack_rows(jnp.dot(onehot, hn, preferred_element_type=F32))

    n_copies = meta_ref[0, 0]

    def issue(j, carry):
        src = pl.multiple_of(meta_ref[0, 1 + j], RUN_CHUNK)
        dst = pl.multiple_of(meta_ref[0, 1 + MAX_COPIES + j], ROW_ALIGN)
        _run_copy(srt_ref, src, xs_ref, dst, sem).start()
        return carry

    lax.fori_loop(0, n_copies, issue, 0)

    def drain(j, carry):
        _run_copy(srt_ref, 0, xs_ref, 0, sem).wait()
        return carry

    lax.fori_loop(0, n_copies, drain, 0)


def _dispatch(hns, meta, post, block_valid, n_rows_padded, bm):
    tiles = [hn.shape[0] // TOKEN_TILE for hn in hns]
    tile_starts = tuple(sum(tiles[:k]) for k in range(len(tiles)))

    def hn_spec(start, count):
        return pl.BlockSpec((TOKEN_TILE, D_MODEL),
                            lambda i, bv: (jnp.clip(i - start, 0, count - 1), 0))

    in_specs = [
        pl.BlockSpec((None, 1, META_WIDTH), lambda i, bv: (i, 0, 0), memory_space=pltpu.SMEM),
        pl.BlockSpec((None, TOP_K, TOKEN_TILE), lambda i, bv: (i, 0, 0)),
    ] + [hn_spec(s, c) for s, c in zip(tile_starts, tiles)]
    grid_spec = pltpu.PrefetchScalarGridSpec(
        num_scalar_prefetch=1,
        grid=(sum(tiles),),
        in_specs=in_specs,
        out_specs=pl.BlockSpec(memory_space=pl.ANY),
        scratch_shapes=[pltpu.VMEM((SORTED_ROWS, PACKED), U32),
                        pltpu.VMEM((bm, PACKED), U32),
                        pltpu.SemaphoreType.DMA],
    )
    return pl.pallas_call(
        functools.partial(_dispatch_kernel, bm=bm, tile_starts=tile_starts),
        grid_spec=grid_spec,
        out_shape=jax.ShapeDtypeStruct((n_rows_padded, PACKED), U32),
        compiler_params=pltpu.CompilerParams(
            dimension_semantics=("arbitrary",),
            vmem_limit_bytes=VMEM_LIMIT_BYTES),
        name="dispatch",
    )(block_valid, meta, post, *hns)


def _expert_kernel(be_ref, bv_ref, xs_ref, wgu_ref, bgu_ref, wd_ref, bd_ref, y_ref, *, bm):
    valid = bv_ref[pl.program_id(0)]

    @pl.when(valid > 0)
    def _():
        row = lax.broadcasted_iota(I32, (bm, PACKED), 0)
        w = xs_ref[...]
        lo, hi = _unpack_rows(jnp.where(row < valid, w, jnp.zeros_like(w)))
        x = jnp.concatenate([lo, hi], axis=1)
        gu = jnp.dot(x, wgu_ref[...], preferred_element_type=F32) + bgu_ref[...]
        glu = jnp.minimum(gu[:, :D_FF], SWIGLU_LIMIT)
        lin = jnp.clip(gu[:, D_FF:], -SWIGLU_LIMIT, SWIGLU_LIMIT)
        act = glu * jax.nn.sigmoid(SWIGLU_ALPHA * glu) * (lin + 1.0)
        y = jnp.dot(act.astype(BF16), wd_ref[...], preferred_element_type=F32) + bd_ref[...]
        y_ref[...] = _pack_rows(y.astype(BF16).astype(F32))

    @pl.when(valid <= 0)
    def _():
        y_ref[...] = jnp.zeros(y_ref.shape, U32)


def _experts(xs, block_expert, block_valid, wgu, bgu, wd, bd, bm):
    n_blocks = xs.shape[0] // bm
    grid_spec = pltpu.PrefetchScalarGridSpec(
        num_scalar_prefetch=2,
        grid=(n_blocks,),
        in_specs=[
            pl.BlockSpec((bm, PACKED), lambda b, be, bv: (b, 0)),
            pl.BlockSpec((None, D_MODEL, 2 * D_FF), lambda b, be, bv: (be[b], 0, 0)),
            pl.BlockSpec((None, 1, 2 * D_FF), lambda b, be, bv: (be[b], 0, 0)),
            pl.BlockSpec((None, D_FF, D_MODEL), lambda b, be, bv: (be[b], 0, 0)),
            pl.BlockSpec((None, 1, D_MODEL), lambda b, be, bv: (be[b], 0, 0)),
        ],
        out_specs=pl.BlockSpec((bm, PACKED), lambda b, be, bv: (b, 0)),
    )
    return pl.pallas_call(
        functools.partial(_expert_kernel, bm=bm),
        grid_spec=grid_spec,
        out_shape=jax.ShapeDtypeStruct(xs.shape, U32),
        compiler_params=pltpu.CompilerParams(
            dimension_semantics=("arbitrary",),
            vmem_limit_bytes=VMEM_LIMIT_BYTES),
        name="experts",
    )(block_expert, block_valid, xs, wgu, bgu, wd, bd)


def _combine_kernel(meta_ref, pos_ref, gate_ref, x1_ref, gfin_ref, y_ref, o_ref, buf_ref, sem):
    @pl.when(pl.program_id(0) == 0)
    def _():
        buf_ref[...] = jnp.zeros(buf_ref.shape, U32)

    n_copies = meta_ref[0, 0]

    def issue(j, carry):
        dst = pl.multiple_of(meta_ref[0, 1 + j], RUN_CHUNK)
        src = pl.multiple_of(meta_ref[0, 1 + MAX_COPIES + j], ROW_ALIGN)
        _run_copy(y_ref, src, buf_ref, dst, sem).start()
        return carry

    lax.fori_loop(0, n_copies, issue, 0)

    def drain(j, carry):
        _run_copy(y_ref, 0, buf_ref, 0, sem).wait()
        return carry

    lax.fori_loop(0, n_copies, drain, 0)

    slot = lax.broadcasted_iota(I32, (TOKEN_TILE, SORTED_ROWS), 1)
    pos = pos_ref[...]
    gates = gate_ref[...]
    g = jnp.zeros((TOKEN_TILE, SORTED_ROWS), F32)
    for k in range(TOP_K):
        g = jnp.where(slot == pos[:, k:k + 1], gates[:, k:k + 1], g)
    g = g.astype(BF16)
    lo, hi = _unpack_rows(buf_ref[...])
    moe = jnp.concatenate([jnp.dot(g, lo, preferred_element_type=F32),
                           jnp.dot(g, hi, preferred_element_type=F32)], axis=1)
    o_ref[...] = _rms(x1_ref[...] + moe, gfin_ref[...])


def _combine(yrows, meta, pos, gates, x1, g_final):
    tokens = x1.shape[0]
    grid_spec = pltpu.PrefetchScalarGridSpec(
        num_scalar_prefetch=0,
        grid=(tokens // TOKEN_TILE,),
        in_specs=[
            pl.BlockSpec((None, 1, META_WIDTH), lambda i: (i, 0, 0), memory_space=pltpu.SMEM),
            pl.BlockSpec((TOKEN_TILE, TOP_K), lambda i: (i, 0)),
            pl.BlockSpec((TOKEN_TILE, TOP_K), lambda i: (i, 0)),
            pl.BlockSpec((TOKEN_TILE, D_MODEL), lambda i: (i, 0)),
            pl.BlockSpec((1, D_MODEL), lambda i: (0, 0)),
            pl.BlockSpec(memory_space=pl.ANY),
        ],
        out_specs=pl.BlockSpec((TOKEN_TILE, D_MODEL), lambda i: (i, 0)),
        scratch_shapes=[pltpu.VMEM((SORTED_ROWS, PACKED), U32), pltpu.SemaphoreType.DMA],
    )
    return pl.pallas_call(
        _combine_kernel,
        grid_spec=grid_spec,
        out_shape=jax.ShapeDtypeStruct((tokens, D_MODEL), F32),
        compiler_params=pltpu.CompilerParams(
            dimension_semantics=("arbitrary",),
            vmem_limit_bytes=VMEM_LIMIT_BYTES),
        name="combine",
    )(meta, pos, gates, x1, g_final, yrows)


def _route(logits, bm):
    tokens = logits.shape[0]
    n_tiles = tokens // TOKEN_TILE
    spare = RUN_CHUNK + bm - 1
    max_rows = tokens * TOP_K + n_tiles * N_EXPERTS * (ROW_ALIGN - 1) + N_EXPERTS * spare
    n_blocks = -(-max_rows // bm)
    top_val, top_idx = lax.top_k(logits, TOP_K)
    gates = jax.nn.softmax(top_val, axis=-1)
    onehot = top_idx[:, :, None] == jnp.arange(N_EXPERTS, dtype=top_idx.dtype)
    sel = jnp.sum(onehot, axis=1, dtype=F32).reshape(n_tiles, TOKEN_TILE, N_EXPERTS)
    earlier = jnp.tril(jnp.ones((TOKEN_TILE, TOKEN_TILE), F32), -1)
    rank = jnp.einsum('ts,nse->nte', earlier, sel).astype(I32)
    runs = jnp.sum(sel, axis=1).astype(I32)
    runs_aligned = (runs + ROW_ALIGN - 1) // ROW_ALIGN * ROW_ALIGN
    base = jnp.cumsum(runs_aligned, axis=0) - runs_aligned
    counts = jnp.sum(runs_aligned, axis=0)
    padded_counts = (counts + spare) // bm * bm
    padded_end = jnp.cumsum(padded_counts)
    padded_start = padded_end - padded_counts
    runs_padded = (runs + RUN_CHUNK - 1) // RUN_CHUNK * RUN_CHUNK
    seg = jnp.cumsum(runs_padded, axis=1) - runs_padded
    slot_of = (seg[:, None, :] + rank).reshape(tokens, 1, N_EXPERTS)
    pos = jnp.sum(jnp.where(onehot, slot_of, 0), axis=-1).astype(I32)

    n_chunks = runs_padded // RUN_CHUNK
    chunk_end = jnp.cumsum(n_chunks, axis=1)
    j = jnp.arange(MAX_COPIES, dtype=I32)
    e_of = jnp.minimum(jnp.sum(j[None, :, None] >= chunk_end[:, None, :], axis=-1), N_EXPERTS - 1)
    pick = lambda a: jnp.take_along_axis(a, e_of, axis=1)
    within = (j[None, :] - pick(chunk_end - n_chunks)) * RUN_CHUNK
    live = j[None, :] < chunk_end[:, -1:]
    src = jnp.where(live, pick(seg) + within, 0)
    dst = jnp.where(live, pick(padded_start[None, :] + base) + within, 0)
    meta = jnp.concatenate([chunk_end[:, -1:], src, dst], axis=1).astype(I32)
    meta = jnp.pad(meta, ((0, 0), (0, META_WIDTH - meta.shape[1]))).reshape(n_tiles, 1, META_WIDTH)

    block_start = jnp.arange(n_blocks, dtype=I32) * bm
    block_expert = jnp.minimum(jnp.searchsorted(padded_end, block_start, side='right'),
                               N_EXPERTS - 1).astype(I32)
    block_valid = jnp.clip(counts[block_expert] - (block_start - padded_start[block_expert]), 0, bm)
    post = pos.reshape(n_tiles, TOKEN_TILE, TOP_K).transpose(0, 2, 1)
    return gates, pos, post, meta, block_expert, block_valid.astype(I32), n_blocks


def _token_mixer(x, p):
    batch, seq, _ = x.shape
    t = _tiles(batch, seq)
    x2 = x.reshape(batch * seq, D_MODEL)
    proj = _inproj(x2, p["g_mix"], p["w_in"], p["g_sgu"], t["tm_in"])
    att = _attention(proj.reshape(batch, seq, IN_COLS), p["attn_scalars"], p["g_subln"],
                     t["tq"], t["tk"])
    return _mix(proj, att.reshape(batch * seq, D_MODEL), x2, p["w_spatial"], p["b_spatial"],
                p["w_branch_sgu"], p["w_branch_attn"], p["w_out"], p["g_ffn"],
                p["w_router"], p["b_router"], t["tm_mix"])


def _moe(mixed, p):
    bm = ROUTE_BLOCK
    logits = jnp.concatenate([lg[:, :N_EXPERTS] for _, _, lg in mixed], axis=0)
    gates, pos, post, meta, block_expert, block_valid, n_blocks = _route(logits, bm)
    xs = _dispatch([hn for _, hn, _ in mixed], meta, post, block_valid, n_blocks * bm, bm)
    tile0, spans = 0, []
    for x1, _, _ in mixed:
        spans.append((tile0, tile0 + x1.shape[0] // TOKEN_TILE))
        tile0 = spans[-1][1]
    yrows = _experts(xs, block_expert, block_valid, p["w_gate_up"], p["b_gate_up"],
                     p["w_down"], p["b_down"], bm)
    outs = []
    for (x1, _, _), (t0, t1) in zip(mixed, spans):
        tok = slice(t0 * TOKEN_TILE, t1 * TOKEN_TILE)
        outs.append(_combine(yrows, meta[t0:t1], pos[tok], gates[tok], x1, p["g_final"]))
    return outs


def _prepare_params(g_mix, w_in, g_sgu, w_spatial, b_spatial, lambda_q1, lambda_k1, lambda_q2,
                    lambda_k2, g_subln, w_branch_sgu, w_branch_attn, w_out, g_ffn, w_router,
                    b_router, w_gate_up, b_gate_up, w_down, b_down, g_final):
    lam = (jnp.exp(jnp.sum(lambda_q1[0].astype(F32) * lambda_k1[0].astype(F32)))
           - jnp.exp(jnp.sum(lambda_q2[0].astype(F32) * lambda_k2[0].astype(F32)))
           + LAMBDA_INIT)
    slopes = jnp.exp2(-8.0 * jnp.arange(1, ATT_HEADS + 1, dtype=F32) / ATT_HEADS) * LOG2E
    pad = LANES - N_EXPERTS
    return dict(
        g_mix=g_mix[0].reshape(1, D_MODEL),
        w_in=w_in[0].astype(BF16),
        g_sgu=g_sgu[0].reshape(1, D_MODEL),
        w_spatial=w_spatial[0].astype(BF16),
        b_spatial=jnp.broadcast_to(b_spatial[0][:, :, None], (SGU_GROUPS, CHUNK, CHUNK)).astype(F32),
        attn_scalars=jnp.concatenate([slopes, lam.reshape(1)]).astype(F32),
        g_subln=g_subln[0].reshape(1, ATT_VDIM),
        w_branch_sgu=w_branch_sgu[0].astype(BF16),
        w_branch_attn=w_branch_attn[0].astype(BF16),
        w_out=w_out[0].astype(BF16),
        g_ffn=g_ffn[0].reshape(1, D_MODEL),
        w_router=jnp.pad(w_router[0], ((0, 0), (0, pad))).astype(BF16),
        b_router=jnp.pad(b_router[0], (0, pad)).reshape(1, LANES).astype(F32),
        w_gate_up=w_gate_up[0].astype(BF16),
        b_gate_up=b_gate_up[0].reshape(N_EXPERTS, 1, 2 * D_FF),
        w_down=w_down[0].astype(BF16),
        b_down=b_down[0].reshape(N_EXPERTS, 1, D_MODEL),
        g_final=g_final.reshape(1, D_MODEL),
    )


def _encode(xs, p):
    outs = _moe([_token_mixer(x, p) for x in xs], p)
    return tuple(o.reshape(x.shape) for o, x in zip(outs, xs))


def kernel(x_prompt, x_sample, g_mix, w_in, g_sgu, w_spatial, b_spatial, lambda_q1, lambda_k1, lambda_q2, lambda_k2, g_subln, w_branch_sgu, w_branch_attn, w_out, g_ffn, w_router, b_router, w_gate_up, b_gate_up, w_down, b_down, g_final):
    p = _prepare_params(g_mix, w_in, g_sgu, w_spatial, b_spatial, lambda_q1, lambda_k1, lambda_q2,
                        lambda_k2, g_subln, w_branch_sgu, w_branch_attn, w_out, g_ffn, w_router,
                        b_router, w_gate_up, b_gate_up, w_down, b_down, g_final)
    return _encode((x_prompt, x_sample), p)
```

```python
import functools
import math

import jax
import jax.numpy as jnp
from jax import lax
from jax.experimental import pallas as pl
from jax.experimental.pallas import tpu as pltpu

D_MODEL = 1024
SGU_GROUPS = 8
CHUNK = 128
ATT_HEADS = 8
ATT_HEAD_DIM = 64
ATT_VDIM = 2 * ATT_HEAD_DIM
N_EXPERTS = 32
TOP_K = 4
D_FF = D_MODEL
SWIGLU_LIMIT = 7.0
SWIGLU_ALPHA = 1.702
RMS_EPS = 1e-5
LAMBDA_INIT = 0.8 - 0.6 * math.exp(-0.3 * 0)
LOG2E = 1.4426950408889634

COL_U, COL_V, COL_Q, COL_K, COL_VA, COL_GS, COL_GA = range(7)
IN_COLS = 7 * D_MODEL
LANES = 128
HEAD_BLOCKS = D_MODEL // LANES

VMEM_LIMIT_BYTES = 56 * 1024 * 1024

F32 = jnp.float32
BF16 = jnp.bfloat16
U32 = jnp.uint32
I32 = jnp.int32


def _tiles(batch, seq):
    tokens = batch * seq
    return dict(
        tm_in=min(1024, tokens),
        tq=min(256, seq),
        tk=min(512, seq),
        tm_mix=min(512, tokens),
    )


MATMUL_ROWS = 256
ROUTE_BLOCK = 256
TOKEN_TILE = 256
RUN_CHUNK = 16
ROW_ALIGN = 8
SORTED_ROWS = -(-(TOKEN_TILE * TOP_K + N_EXPERTS * (RUN_CHUNK - 1)) // LANES) * LANES
MAX_COPIES = SORTED_ROWS // RUN_CHUNK
META_WIDTH = -(-(1 + 2 * MAX_COPIES) // LANES) * LANES
PACKED = D_MODEL // 2
HI_MASK = 0xFFFF0000


def _gelu_tanh(x):
    return 0.5 * x * (1.0 + jnp.tanh(math.sqrt(2.0 / math.pi) * (x + 0.044715 * (x * x * x))))


def _rms(x, g):
    r = x * lax.rsqrt(jnp.mean(x * x, axis=-1, keepdims=True) + RMS_EPS)
    return r * g


def _pack_rows(x):
    lo = lax.bitcast_convert_type(x[:, :PACKED], U32) >> jnp.uint32(16)
    hi = lax.bitcast_convert_type(x[:, PACKED:], U32) & jnp.uint32(HI_MASK)
    return lo | hi


def _unpack_rows(w):
    lo = lax.bitcast_convert_type(w << jnp.uint32(16), F32).astype(BF16)
    hi = lax.bitcast_convert_type(w & jnp.uint32(HI_MASK), F32).astype(BF16)
    return lo, hi


def _inproj_kernel(x_ref, gmix_ref, w_ref, gsgu_ref, o_ref, xn_ref, *, tm):
    j = pl.program_id(1)

    @pl.when(j == 0)
    def _():
        xn_ref[...] = _rms(x_ref[...], gmix_ref[...]).astype(BF16)

    def project(epilogue):
        for c in range(tm // MATMUL_ROWS):
            rows = slice(c * MATMUL_ROWS, (c + 1) * MATMUL_ROWS)
            y = jnp.dot(xn_ref[rows, :], w_ref[...], preferred_element_type=F32)
            o_ref[rows, :] = epilogue(y).astype(BF16)

    @pl.when(j == COL_U)
    def _():
        project(_gelu_tanh)

    @pl.when(j == COL_V)
    def _():
        project(lambda y: _rms(_gelu_tanh(y), gsgu_ref[...]))

    @pl.when(jnp.logical_and(j >= COL_Q, j <= COL_VA))
    def _():
        project(lambda y: y)

    @pl.when(j >= COL_GS)
    def _():
        project(jax.nn.sigmoid)


def _inproj(x2, g_mix, w_in, g_sgu, tm):
    tokens = x2.shape[0]
    return pl.pallas_call(
        functools.partial(_inproj_kernel, tm=tm),
        grid=(tokens // tm, IN_COLS // D_MODEL),
        in_specs=[
            pl.BlockSpec((tm, D_MODEL), lambda i, j: (i, 0)),
            pl.BlockSpec((1, D_MODEL), lambda i, j: (0, 0)),
            pl.BlockSpec((D_MODEL, D_MODEL), lambda i, j: (0, j)),
            pl.BlockSpec((1, D_MODEL), lambda i, j: (0, 0)),
        ],
        out_specs=pl.BlockSpec((tm, D_MODEL), lambda i, j: (i, j)),
        out_shape=jax.ShapeDtypeStruct((tokens, IN_COLS), BF16),
        scratch_shapes=[pltpu.VMEM((tm, D_MODEL), BF16)],
        compiler_params=pltpu.CompilerParams(
            dimension_semantics=("arbitrary", "arbitrary"),
            vmem_limit_bytes=VMEM_LIMIT_BYTES),
        name="inproj",
    )(x2, g_mix, w_in, g_sgu)


def _attn_kernel(sc_ref, q_ref, k_ref, v_ref, g_ref, o_ref,
                 qs_ref, tbl_ref, sb_ref, pm_ref, m_ref, l_ref, acc_ref, *, tq, tk, seq):
    h = pl.program_id(1)
    qi = pl.program_id(2)
    slope = sc_ref[h]
    lam = sc_ref[ATT_HEADS]
    nk = seq // tk
    n_diag = tk // tq
    rows = 2 * tq
    lane_tiles = tk // LANES

    @pl.when(qi == 0)
    def _():
        r = lax.broadcasted_iota(jnp.int32, (rows, tk), 0)
        c = lax.broadcasted_iota(jnp.int32, (rows, tk), 1)
        d0 = (jnp.where(r >= tq, r - tq, r) - c).astype(F32)
        tbl_ref[0] = slope * d0
        tbl_ref[1] = -slope * d0
        for d in range(n_diag):
            tbl_ref[2 + d] = slope * jnp.abs(d0 + float(d * tq))

    q = (q_ref[...].astype(F32) * (ATT_HEAD_DIM ** -0.5 * LOG2E)).astype(BF16)
    lane = lax.broadcasted_iota(jnp.int32, (tq, LANES), 1)
    zero = jnp.zeros_like(q)
    qs_ref[0:tq, :] = jnp.where(lane < ATT_HEAD_DIM, q, zero)
    qs_ref[tq:rows, :] = jnp.where(lane >= ATT_HEAD_DIM, q, zero)

    m_ref[...] = jnp.full(m_ref.shape, -1e30, F32)
    l_ref[...] = jnp.zeros(l_ref.shape, F32)
    acc_ref[...] = jnp.zeros(acc_ref.shape, F32)
    q0 = qi * tq
    kd = q0 // tk

    def tile_consts(kj):
        delta = (q0 - kj * tk).astype(F32)
        left, right = kj < kd, kj > kd
        sel = jnp.where(left, 0, jnp.where(right, 1, 2 + (q0 - kd * tk) // tq))
        cst = jnp.where(left, slope * delta, jnp.where(right, -slope * delta, 0.0))
        return sel, cst

    def scores(kj, slot):
        sel, _ = tile_consts(kj)
        kt = k_ref[pl.ds(pl.multiple_of(kj * tk, tk), tk), :]
        s = lax.dot_general(qs_ref[...], kt, (((1,), (1,)), ((), ())),
                            preferred_element_type=F32)
        pm = None
        for c in range(lane_tiles):
            cols = slice(c * LANES, (c + 1) * LANES)
            sb = s[:, cols] - tbl_ref[sel, :, cols]
            sb_ref[slot, :, cols] = sb
            pm = sb if pm is None else jnp.maximum(pm, sb)
        pm_ref[slot] = pm

    def accumulate(kj, slot):
        _, cst = tile_consts(kj)
        red = jnp.max(pm_ref[slot], axis=-1, keepdims=True) - cst
        m_old = m_ref[...]
        m_new = jnp.maximum(m_old, jnp.broadcast_to(red, m_old.shape))
        alpha = jnp.exp2(m_old - m_new)
        m_ref[...] = m_new
        shift = m_new + cst
        ps, lsum = [], None
        for c in range(lane_tiles):
            p = jnp.exp2(sb_ref[slot, :, c * LANES:(c + 1) * LANES] - shift)
            lsum = p if lsum is None else lsum + p
            ps.append(p.astype(BF16))
        l_ref[...] = alpha * l_ref[...] + lsum
        vt = v_ref[pl.ds(pl.multiple_of(kj * tk, tk), tk), :]
        pv = jnp.dot(jnp.concatenate(ps, axis=1), vt, preferred_element_type=F32)
        acc_ref[...] = alpha * acc_ref[...] + pv

    scores(0, 0)
    for kj in range(nk):
        if kj + 1 < nk:
            scores(kj + 1, (kj + 1) % 2)
        accumulate(kj, kj % 2)

    o = acc_ref[...] / jnp.sum(l_ref[...], axis=-1, keepdims=True)
    d = o[0:tq, :] - lam * o[tq:rows, :]
    o_ref[...] = (_rms(d, g_ref[...]) * (1.0 - LAMBDA_INIT)).astype(BF16)


def _attention(proj3, scal, g_subln, tq, tk):
    batch, seq, _ = proj3.shape
    kern = functools.partial(_attn_kernel, tq=tq, tk=tk, seq=seq)
    qb, kb, vb = COL_Q * HEAD_BLOCKS, COL_K * HEAD_BLOCKS, COL_VA * HEAD_BLOCKS
    return pl.pallas_call(
        kern,
        grid=(batch, ATT_HEADS, seq // tq),
        in_specs=[
            pl.BlockSpec(memory_space=pltpu.SMEM),
            pl.BlockSpec((None, tq, LANES), lambda b, h, i: (b, i, qb + h)),
            pl.BlockSpec((None, seq, LANES), lambda b, h, i: (b, 0, kb + h)),
            pl.BlockSpec((None, seq, LANES), lambda b, h, i: (b, 0, vb + h)),
            pl.BlockSpec((1, LANES), lambda b, h, i: (0, 0)),
        ],
        out_specs=pl.BlockSpec((None, tq, LANES), lambda b, h, i: (b, i, h)),
        out_shape=jax.ShapeDtypeStruct((batch, seq, D_MODEL), BF16),
        scratch_shapes=[
            pltpu.VMEM((2 * tq, LANES), BF16),
            pltpu.VMEM((2 + tk // tq, 2 * tq, tk), F32),
            pltpu.VMEM((2, 2 * tq, tk), F32),
            pltpu.VMEM((2, 2 * tq, LANES), F32),
            pltpu.VMEM((2 * tq, LANES), F32),
            pltpu.VMEM((2 * tq, LANES), F32),
            pltpu.VMEM((2 * tq, LANES), F32),
        ],
        compiler_params=pltpu.CompilerParams(
            dimension_semantics=("arbitrary", "arbitrary", "arbitrary"),
            vmem_limit_bytes=VMEM_LIMIT_BYTES),
        name="diffattn",
    )(scal, proj3, proj3, proj3, g_subln)


def _mix_kernel(u_ref, vn_ref, gs_ref, ga_ref, att_ref, x_ref, ws_ref, bs_ref,
                wbs_ref, wba_ref, wo_ref, gffn_ref, wr_ref, br_ref,
                x1_ref, hn_ref, lg_ref, sgu_ref, *, tm):
    for n in range(tm // CHUNK):
        rows = slice(n * CHUNK, (n + 1) * CHUNK)
        for g in range(SGU_GROUPS):
            cols = slice(g * CHUNK, (g + 1) * CHUNK)
            vm = jnp.dot(ws_ref[g], vn_ref[rows, cols], preferred_element_type=F32) + bs_ref[g]
            sgu_ref[rows, cols] = (u_ref[rows, cols].astype(F32) * vm).astype(BF16)

    a = jnp.dot(sgu_ref[...], wbs_ref[...], preferred_element_type=F32)
    b = jnp.dot(att_ref[...], wba_ref[...], preferred_element_type=F32)
    merged = gs_ref[...].astype(F32) * a + ga_ref[...].astype(F32) * b
    x1 = x_ref[...] + jnp.dot(merged.astype(BF16), wo_ref[...], preferred_element_type=F32)
    x1_ref[...] = x1
    hn = _rms(x1, gffn_ref[...]).astype(BF16)
    hn_ref[...] = hn
    lg_ref[...] = jnp.dot(hn, wr_ref[...], preferred_element_type=F32) + br_ref[...]


def _mix(proj, att, x2, ws, bs, wbs, wba, wo, g_ffn, wr, br, tm):
    tokens = x2.shape[0]
    row = lambda c: pl.BlockSpec((tm, D_MODEL), lambda i: (i, c))
    full = lambda shape: pl.BlockSpec(shape, lambda i: (0,) * len(shape))
    return pl.pallas_call(
        functools.partial(_mix_kernel, tm=tm),
        grid=(tokens // tm,),
        in_specs=[
            row(COL_U), row(COL_V), row(COL_GS), row(COL_GA),
            row(0), row(0),
            full((SGU_GROUPS, CHUNK, CHUNK)), full((SGU_GROUPS, CHUNK, CHUNK)),
            full((D_MODEL, D_MODEL)), full((D_MODEL, D_MODEL)), full((D_MODEL, D_MODEL)),
            full((1, D_MODEL)), full((D_MODEL, LANES)), full((1, LANES)),
        ],
        out_specs=[row(0), row(0), pl.BlockSpec((tm, LANES), lambda i: (i, 0))],
        out_shape=[
            jax.ShapeDtypeStruct((tokens, D_MODEL), F32),
            jax.ShapeDtypeStruct((tokens, D_MODEL), BF16),
            jax.ShapeDtypeStruct((tokens, LANES), F32),
        ],
        scratch_shapes=[pltpu.VMEM((tm, D_MODEL), BF16)],
        compiler_params=pltpu.CompilerParams(
            dimension_semantics=("arbitrary",),
            vmem_limit_bytes=VMEM_LIMIT_BYTES),
        name="mix",
    )(proj, proj, proj, proj, att, x2, ws, bs, wbs, wba, wo, g_ffn, wr, br)


def _run_copy(src_ref, src_row, dst_ref, dst_row, sem):
    return pltpu.make_async_copy(src_ref.at[pl.ds(src_row, RUN_CHUNK)],
                                 dst_ref.at[pl.ds(dst_row, RUN_CHUNK)], sem)


def _dispatch_kernel(bv_ref, meta_ref, post_ref, *rest, bm, tile_starts):
    hn_refs = rest[:len(tile_starts)]
    xs_ref, srt_ref, zero_ref, pending_ref, sem = rest[len(tile_starts):]
    i = pl.program_id(0)

    @pl.when(i == 0)
    def _():
        zero_ref[...] = jnp.zeros(zero_ref.shape, U32)

        def maybe_zero(b, carry):
            @pl.when(bv_ref[b] < bm)
            def _():
                cp = pltpu.make_async_copy(
                    zero_ref, xs_ref.at[pl.ds(pl.multiple_of(b * bm, bm), bm)], sem)
                cp.start()
                cp.wait()
            return carry

        lax.fori_loop(0, xs_ref.shape[0] // bm, maybe_zero, 0)

    hn = hn_refs[0][...]
    for ref, start in zip(hn_refs[1:], tile_starts[1:]):
        hn = jnp.where(i >= start, ref[...], hn)

    slot = lax.broadcasted_iota(I32, (SORTED_ROWS, TOKEN_TILE), 0)
    hit = slot == post_ref[0:1, :]
    for k in range(1, TOP_K):
        hit = jnp.logical_or(hit, slot == post_ref[k:k + 1, :])
    onehot = jnp.where(hit, 1.0, 0.0).astype(BF16)
    buf = srt_ref.at[i % 2]
    buf[...] = _pack_rows(jnp.dot(onehot, hn, preferred_element_type=F32))

    def drain(count):
        def wait_one(j, carry):
            _run_copy(buf, 0, xs_ref, 0, sem).wait()
            return carry
        lax.fori_loop(0, count, wait_one, 0)

    @pl.when(i > 0)
    def _():
        drain(pending_ref[0])

    n_copies = meta_ref[0, 0]

    def issue(j, carry):
        src = pl.multiple_of(meta_ref[0, 1 + j], RUN_CHUNK)
        dst = pl.multiple_of(meta_ref[0, 1 + MAX_COPIES + j], ROW_ALIGN)
        _run_copy(buf, src, xs_ref, dst, sem).start()
        return carry

    lax.fori_loop(0, n_copies, issue, 0)
    pending_ref[0] = n_copies

    @pl.when(i == pl.num_programs(0) - 1)
    def _():
        drain(n_copies)


def _dispatch(hns, meta, post, block_valid, n_rows_padded, bm):
    tiles = [hn.shape[0] // TOKEN_TILE for hn in hns]
    tile_starts = tuple(sum(tiles[:k]) for k in range(len(tiles)))

    def hn_spec(start, count):
        return pl.BlockSpec((TOKEN_TILE, D_MODEL),
                            lambda i, bv: (jnp.clip(i - start, 0, count - 1), 0))

    in_specs = [
        pl.BlockSpec((None, 1, META_WIDTH), lambda i, bv: (i, 0, 0), memory_space=pltpu.SMEM),
        pl.BlockSpec((None, TOP_K, TOKEN_TILE), lambda i, bv: (i, 0, 0)),
    ] + [hn_spec(s, c) for s, c in zip(tile_starts, tiles)]
    grid_spec = pltpu.PrefetchScalarGridSpec(
        num_scalar_prefetch=1,
        grid=(sum(tiles),),
        in_specs=in_specs,
        out_specs=pl.BlockSpec(memory_space=pl.ANY),
        scratch_shapes=[pltpu.VMEM((2, SORTED_ROWS, PACKED), U32),
                        pltpu.VMEM((bm, PACKED), U32),
                        pltpu.SMEM((1,), I32),
                        pltpu.SemaphoreType.DMA],
    )
    return pl.pallas_call(
        functools.partial(_dispatch_kernel, bm=bm, tile_starts=tile_starts),
        grid_spec=grid_spec,
        out_shape=jax.ShapeDtypeStruct((n_rows_padded, PACKED), U32),
        compiler_params=pltpu.CompilerParams(
            dimension_semantics=("arbitrary",),
            vmem_limit_bytes=VMEM_LIMIT_BYTES),
        name="dispatch",
    )(block_valid, meta, post, *hns)


def _expert_kernel(be_ref, bv_ref, xs_ref, wgu_ref, bgu_ref, wd_ref, bd_ref, y_ref,
                   wgu_bf_ref, wd_bf_ref, *, bm):
    b = pl.program_id(0)
    valid = bv_ref[b]

    fresh = jnp.logical_or(b == 0, be_ref[b] != be_ref[jnp.maximum(b - 1, 0)])

    @pl.when(jnp.logical_and(valid > 0, fresh))
    def _():
        wgu_bf_ref[...] = wgu_ref[...].astype(BF16)
        wd_bf_ref[...] = wd_ref[...].astype(BF16)

    @pl.when(valid > 0)
    def _():
        row = lax.broadcasted_iota(I32, (bm, PACKED), 0)
        w = xs_ref[...]
        lo, hi = _unpack_rows(jnp.where(row < valid, w, jnp.zeros_like(w)))
        x = jnp.concatenate([lo, hi], axis=1)
        gu = jnp.dot(x, wgu_bf_ref[...], preferred_element_type=F32) + bgu_ref[...]
        glu = jnp.minimum(gu[:, :D_FF], SWIGLU_LIMIT)
        lin = jnp.clip(gu[:, D_FF:], -SWIGLU_LIMIT, SWIGLU_LIMIT)
        act = glu * jax.nn.sigmoid(SWIGLU_ALPHA * glu) * (lin + 1.0)
        y = jnp.dot(act.astype(BF16), wd_bf_ref[...], preferred_element_type=F32) + bd_ref[...]
        y_ref[...] = _pack_rows(y.astype(BF16).astype(F32))

    @pl.when(valid <= 0)
    def _():
        y_ref[...] = jnp.zeros(y_ref.shape, U32)


def _experts(xs, block_expert, block_valid, wgu, bgu, wd, bd, bm):
    n_blocks = xs.shape[0] // bm
    grid_spec = pltpu.PrefetchScalarGridSpec(
        num_scalar_prefetch=2,
        grid=(n_blocks,),
        in_specs=[
            pl.BlockSpec((bm, PACKED), lambda b, be, bv: (b, 0)),
            pl.BlockSpec((None, D_MODEL, 2 * D_FF), lambda b, be, bv: (be[b], 0, 0)),
            pl.BlockSpec((None, 1, 2 * D_FF), lambda b, be, bv: (be[b], 0, 0)),
            pl.BlockSpec((None, D_FF, D_MODEL), lambda b, be, bv: (be[b], 0, 0)),
            pl.BlockSpec((None, 1, D_MODEL), lambda b, be, bv: (be[b], 0, 0)),
        ],
        out_specs=pl.BlockSpec((bm, PACKED), lambda b, be, bv: (b, 0)),
        scratch_shapes=[pltpu.VMEM((D_MODEL, 2 * D_FF), BF16), pltpu.VMEM((D_FF, D_MODEL), BF16)],
    )
    return pl.pallas_call(
        functools.partial(_expert_kernel, bm=bm),
        grid_spec=grid_spec,
        out_shape=jax.ShapeDtypeStruct(xs.shape, U32),
        compiler_params=pltpu.CompilerParams(
            dimension_semantics=("arbitrary",),
            vmem_limit_bytes=VMEM_LIMIT_BYTES),
        name="experts",
    )(block_expert, block_valid, xs, wgu, bgu, wd, bd)


def _combine_kernel(meta_ref, pos_ref, gate_ref, x1_ref, gfin_ref, y_ref, o_ref, buf_ref, sem):
    @pl.when(pl.program_id(0) == 0)
    def _():
        buf_ref[...] = jnp.zeros(buf_ref.shape, U32)

    n_copies = meta_ref[0, 0]

    def issue(j, carry):
        dst = pl.multiple_of(meta_ref[0, 1 + j], RUN_CHUNK)
        src = pl.multiple_of(meta_ref[0, 1 + MAX_COPIES + j], ROW_ALIGN)
        _run_copy(y_ref, src, buf_ref, dst, sem).start()
        return carry

    lax.fori_loop(0, n_copies, issue, 0)

    def drain(j, carry):
        _run_copy(y_ref, 0, buf_ref, 0, sem).wait()
        return carry

    lax.fori_loop(0, n_copies, drain, 0)

    slot = lax.broadcasted_iota(I32, (TOKEN_TILE, SORTED_ROWS), 1)
    pos = pos_ref[...]
    gates = gate_ref[...]
    g = jnp.zeros((TOKEN_TILE, SORTED_ROWS), F32)
    for k in range(TOP_K):
        g = jnp.where(slot == pos[:, k:k + 1], gates[:, k:k + 1], g)
    g = g.astype(BF16)
    lo, hi = _unpack_rows(buf_ref[...])
    moe = jnp.concatenate([jnp.dot(g, lo, preferred_element_type=F32),
                           jnp.dot(g, hi, preferred_element_type=F32)], axis=1)
    o_ref[...] = _rms(x1_ref[...] + moe, gfin_ref[...])


def _combine(yrows, meta, pos, gates, x1, g_final):
    tokens = x1.shape[0]
    grid_spec = pltpu.PrefetchScalarGridSpec(
        num_scalar_prefetch=0,
        grid=(tokens // TOKEN_TILE,),
        in_specs=[
            pl.BlockSpec((None, 1, META_WIDTH), lambda i: (i, 0, 0), memory_space=pltpu.SMEM),
            pl.BlockSpec((TOKEN_TILE, TOP_K), lambda i: (i, 0)),
            pl.BlockSpec((TOKEN_TILE, TOP_K), lambda i: (i, 0)),
            pl.BlockSpec((TOKEN_TILE, D_MODEL), lambda i: (i, 0)),
            pl.BlockSpec((1, D_MODEL), lambda i: (0, 0)),
            pl.BlockSpec(memory_space=pl.ANY),
        ],
        out_specs=pl.BlockSpec((TOKEN_TILE, D_MODEL), lambda i: (i, 0)),
        scratch_shapes=[pltpu.VMEM((SORTED_ROWS, PACKED), U32), pltpu.SemaphoreType.DMA],
    )
    return pl.pallas_call(
        _combine_kernel,
        grid_spec=grid_spec,
        out_shape=jax.ShapeDtypeStruct((tokens, D_MODEL), F32),
        compiler_params=pltpu.CompilerParams(
            dimension_semantics=("arbitrary",),
            vmem_limit_bytes=VMEM_LIMIT_BYTES),
        name="combine",
    )(meta, pos, gates, x1, g_final, yrows)


def _route(logits, bm):
    tokens = logits.shape[0]
    n_tiles = tokens // TOKEN_TILE
    spare = RUN_CHUNK + bm - 1
    max_rows = tokens * TOP_K + n_tiles * N_EXPERTS * (ROW_ALIGN - 1) + N_EXPERTS * spare
    n_blocks = -(-max_rows // bm)
    top_val, top_idx = lax.top_k(logits, TOP_K)
    gates = jax.nn.softmax(top_val, axis=-1)
    onehot = top_idx[:, :, None] == jnp.arange(N_EXPERTS, dtype=top_idx.dtype)
    sel = jnp.sum(onehot, axis=1, dtype=F32).reshape(n_tiles, TOKEN_TILE, N_EXPERTS)
    earlier = jnp.tril(jnp.ones((TOKEN_TILE, TOKEN_TILE), F32), -1)
    rank = jnp.einsum('ts,nse->nte', earlier, sel).astype(I32)
    runs = jnp.sum(sel, axis=1).astype(I32)
    runs_aligned = (runs + ROW_ALIGN - 1) // ROW_ALIGN * ROW_ALIGN
    base = jnp.cumsum(runs_aligned, axis=0) - runs_aligned
    counts = jnp.sum(runs_aligned, axis=0)
    padded_counts = (counts + spare) // bm * bm
    padded_end = jnp.cumsum(padded_counts)
    padded_start = padded_end - padded_counts
    runs_padded = (runs + RUN_CHUNK - 1) // RUN_CHUNK * RUN_CHUNK
    seg = jnp.cumsum(runs_padded, axis=1) - runs_padded
    slot_of = (seg[:, None, :] + rank).reshape(tokens, 1, N_EXPERTS)
    pos = jnp.sum(jnp.where(onehot, slot_of, 0), axis=-1).astype(I32)

    n_chunks = runs_padded // RUN_CHUNK
    chunk_end = jnp.cumsum(n_chunks, axis=1)
    j = jnp.arange(MAX_COPIES, dtype=I32)
    chunk_start = chunk_end - n_chunks
    owns = jnp.logical_and(j[None, :, None] >= chunk_start[:, None, :],
                           j[None, :, None] < chunk_end[:, None, :])
    pick = lambda a: jnp.sum(jnp.where(owns, a[:, None, :], 0), axis=-1)
    within = (j[None, :] - pick(chunk_start)) * RUN_CHUNK
    live = j[None, :] < chunk_end[:, -1:]
    src = jnp.where(live, pick(seg) + within, 0)
    dst = jnp.where(live, pick(padded_start[None, :] + base) + within, 0)
    meta = jnp.concatenate([chunk_end[:, -1:], src, dst], axis=1).astype(I32)
    meta = jnp.pad(meta, ((0, 0), (0, META_WIDTH - meta.shape[1]))).reshape(n_tiles, 1, META_WIDTH)

    block_start = jnp.arange(n_blocks, dtype=I32) * bm
    in_region = jnp.logical_and(block_start[:, None] >= padded_start[None, :],
                                block_start[:, None] < padded_end[None, :])
    expert_ids = jnp.arange(N_EXPERTS, dtype=I32)
    block_expert = jnp.where(jnp.any(in_region, axis=1),
                             jnp.sum(jnp.where(in_region, expert_ids[None, :], 0), axis=1),
                             N_EXPERTS - 1).astype(I32)
    rows_left = jnp.sum(jnp.where(in_region, (padded_start + counts)[None, :], 0), axis=1) - block_start
    block_valid = jnp.clip(rows_left, 0, bm)
    post = pos.reshape(n_tiles, TOKEN_TILE, TOP_K).transpose(0, 2, 1)
    return gates, pos, post, meta, block_expert, block_valid.astype(I32), n_blocks


def _token_mixer(x, p):
    batch, seq, _ = x.shape
    t = _tiles(batch, seq)
    x2 = x.reshape(batch * seq, D_MODEL)
    proj = _inproj(x2, p["g_mix"], p["w_in"], p["g_sgu"], t["tm_in"])
    att = _attention(proj.reshape(batch, seq, IN_COLS), p["attn_scalars"], p["g_subln"],
                     t["tq"], t["tk"])
    return _mix(proj, att.reshape(batch * seq, D_MODEL), x2, p["w_spatial"], p["b_spatial"],
                p["w_branch_sgu"], p["w_branch_attn"], p["w_out"], p["g_ffn"],
                p["w_router"], p["b_router"], t["tm_mix"])


def _moe(mixed, p):
    bm = ROUTE_BLOCK
    logits = jnp.concatenate([lg[:, :N_EXPERTS] for _, _, lg in mixed], axis=0)
    gates, pos, post, meta, block_expert, block_valid, n_blocks = _route(logits, bm)
    xs = _dispatch([hn for _, hn, _ in mixed], meta, post, block_valid, n_blocks * bm, bm)
    tile0, spans = 0, []
    for x1, _, _ in mixed:
        spans.append((tile0, tile0 + x1.shape[0] // TOKEN_TILE))
        tile0 = spans[-1][1]
    yrows = _experts(xs, block_expert, block_valid, p["w_gate_up"], p["b_gate_up"],
                     p["w_down"], p["b_down"], bm)
    outs = []
    for (x1, _, _), (t0, t1) in zip(mixed, spans):
        tok = slice(t0 * TOKEN_TILE, t1 * TOKEN_TILE)
        outs.append(_combine(yrows, meta[t0:t1], pos[tok], gates[tok], x1, p["g_final"]))
    return outs


def _prepare_params(g_mix, w_in, g_sgu, w_spatial, b_spatial, lambda_q1, lambda_k1, lambda_q2,
                    lambda_k2, g_subln, w_branch_sgu, w_branch_attn, w_out, g_ffn, w_router,
                    b_router, w_gate_up, b_gate_up, w_down, b_down, g_final):
    lam = (jnp.exp(jnp.sum(lambda_q1[0].astype(F32) * lambda_k1[0].astype(F32)))
           - jnp.exp(jnp.sum(lambda_q2[0].astype(F32) * lambda_k2[0].astype(F32)))
           + LAMBDA_INIT)
    slopes = jnp.exp2(-8.0 * jnp.arange(1, ATT_HEADS + 1, dtype=F32) / ATT_HEADS) * LOG2E
    pad = LANES - N_EXPERTS
    return dict(
        g_mix=g_mix[0].reshape(1, D_MODEL),
        w_in=w_in[0].astype(BF16),
        g_sgu=g_sgu[0].reshape(1, D_MODEL),
        w_spatial=w_spatial[0].astype(BF16),
        b_spatial=jnp.broadcast_to(b_spatial[0][:, :, None], (SGU_GROUPS, CHUNK, CHUNK)).astype(F32),
        attn_scalars=jnp.concatenate([slopes, lam.reshape(1)]).astype(F32),
        g_subln=g_subln[0].reshape(1, ATT_VDIM),
        w_branch_sgu=w_branch_sgu[0].astype(BF16),
        w_branch_attn=w_branch_attn[0].astype(BF16),
        w_out=w_out[0].astype(BF16),
        g_ffn=g_ffn[0].reshape(1, D_MODEL),
        w_router=jnp.pad(w_router[0], ((0, 0), (0, pad))).astype(BF16),
        b_router=jnp.pad(b_router[0], (0, pad)).reshape(1, LANES).astype(F32),
        w_gate_up=w_gate_up[0],
        b_gate_up=b_gate_up[0].reshape(N_EXPERTS, 1, 2 * D_FF),
        w_down=w_down[0],
        b_down=b_down[0].reshape(N_EXPERTS, 1, D_MODEL),
        g_final=g_final.reshape(1, D_MODEL),
    )


def _encode(xs, p):
    outs = _moe([_token_mixer(x, p) for x in xs], p)
    return tuple(o.reshape(x.shape) for o, x in zip(outs, xs))


def kernel(x_prompt, x_sample, g_mix, w_in, g_sgu, w_spatial, b_spatial, lambda_q1, lambda_k1, lambda_q2, lambda_k2, g_subln, w_branch_sgu, w_branch_attn, w_out, g_ffn, w_router, b_router, w_gate_up, b_gate_up, w_down, b_down, g_final):
    p = _prepare_params(g_mix, w_in, g_sgu, w_spatial, b_spatial, lambda_q1, lambda_k1, lambda_q2,
                        lambda_k2, g_subln, w_branch_sgu, w_branch_attn, w_out, g_ffn, w_router,
                        b_router, w_gate_up, b_gate_up, w_down, b_down, g_final)
    return _encode((x_prompt, x_sample), p)
```

```python
import functools
import math

import jax
import jax.numpy as jnp
from jax import lax
from jax.experimental import pallas as pl
from jax.experimental.pallas import tpu as pltpu

D_MODEL = 1024
SGU_GROUPS = 8
CHUNK = 128
ATT_HEADS = 8
ATT_HEAD_DIM = 64
ATT_VDIM = 2 * ATT_HEAD_DIM
N_EXPERTS = 32
TOP_K = 4
D_FF = D_MODEL
SWIGLU_LIMIT = 7.0
SWIGLU_ALPHA = 1.702
RMS_EPS = 1e-5
LAMBDA_INIT = 0.8 - 0.6 * math.exp(-0.3 * 0)
LOG2E = 1.4426950408889634

COL_U, COL_V, COL_Q, COL_K, COL_VA, COL_GS, COL_GA = range(7)
IN_COLS = 7 * D_MODEL
LANES = 128
HEAD_BLOCKS = D_MODEL // LANES

VMEM_LIMIT_BYTES = 56 * 1024 * 1024

F32 = jnp.float32
BF16 = jnp.bfloat16
U32 = jnp.uint32
I32 = jnp.int32


def _tiles(batch, seq):
    tokens = batch * seq
    return dict(
        tm_in=min(1024, tokens),
        tq=min(256, seq),
        tk=min(512, seq),
        tm_mix=min(512, tokens),
    )


MATMUL_ROWS = 256
ROUTE_BLOCK = 256
TOKEN_TILE = 256
RUN_CHUNK = 16
ROW_ALIGN = 8
SORTED_ROWS = -(-(TOKEN_TILE * TOP_K + N_EXPERTS * (RUN_CHUNK - 1)) // LANES) * LANES
MAX_COPIES = SORTED_ROWS // RUN_CHUNK
META_WIDTH = -(-(1 + 2 * MAX_COPIES) // LANES) * LANES
PACKED = D_MODEL // 2
HI_MASK = 0xFFFF0000


def _gelu_tanh(x):
    return 0.5 * x * (1.0 + jnp.tanh(math.sqrt(2.0 / math.pi) * (x + 0.044715 * (x * x * x))))


def _rms(x, g):
    r = x * lax.rsqrt(jnp.mean(x * x, axis=-1, keepdims=True) + RMS_EPS)
    return r * g


def _pack_rows(x):
    lo = lax.bitcast_convert_type(x[:, :PACKED], U32) >> jnp.uint32(16)
    hi = lax.bitcast_convert_type(x[:, PACKED:], U32) & jnp.uint32(HI_MASK)
    return lo | hi


def _unpack_rows(w):
    lo = lax.bitcast_convert_type(w << jnp.uint32(16), F32).astype(BF16)
    hi = lax.bitcast_convert_type(w & jnp.uint32(HI_MASK), F32).astype(BF16)
    return lo, hi


def _inproj_kernel(x_ref, gmix_ref, w_ref, gsgu_ref, o_ref, xn_ref, *, tm):
    j = pl.program_id(1)

    @pl.when(j == 0)
    def _():
        xn_ref[...] = _rms(x_ref[...], gmix_ref[...]).astype(BF16)

    def project(epilogue):
        for c in range(tm // MATMUL_ROWS):
            rows = slice(c * MATMUL_ROWS, (c + 1) * MATMUL_ROWS)
            y = jnp.dot(xn_ref[rows, :], w_ref[...], preferred_element_type=F32)
            o_ref[rows, :] = epilogue(y).astype(BF16)

    @pl.when(j == COL_U)
    def _():
        project(_gelu_tanh)

    @pl.when(j == COL_V)
    def _():
        project(lambda y: _rms(_gelu_tanh(y), gsgu_ref[...]))

    @pl.when(jnp.logical_and(j >= COL_Q, j <= COL_VA))
    def _():
        project(lambda y: y)

    @pl.when(j >= COL_GS)
    def _():
        project(jax.nn.sigmoid)


def _inproj(x2, g_mix, w_in, g_sgu, tm):
    tokens = x2.shape[0]
    return pl.pallas_call(
        functools.partial(_inproj_kernel, tm=tm),
        grid=(tokens // tm, IN_COLS // D_MODEL),
        in_specs=[
            pl.BlockSpec((tm, D_MODEL), lambda i, j: (i, 0)),
            pl.BlockSpec((1, D_MODEL), lambda i, j: (0, 0)),
            pl.BlockSpec((D_MODEL, D_MODEL), lambda i, j: (0, j)),
            pl.BlockSpec((1, D_MODEL), lambda i, j: (0, 0)),
        ],
        out_specs=pl.BlockSpec((tm, D_MODEL), lambda i, j: (i, j)),
        out_shape=jax.ShapeDtypeStruct((tokens, IN_COLS), BF16),
        scratch_shapes=[pltpu.VMEM((tm, D_MODEL), BF16)],
        compiler_params=pltpu.CompilerParams(
            dimension_semantics=("arbitrary", "arbitrary"),
            vmem_limit_bytes=VMEM_LIMIT_BYTES),
        name="inproj",
    )(x2, g_mix, w_in, g_sgu)


def _attn_kernel(sc_ref, q_ref, k_ref, v_ref, g_ref, o_ref,
                 qs_ref, tbl_ref, sb_ref, pm_ref, m_ref, l_ref, acc_ref, *, tq, tk, seq):
    h = pl.program_id(1)
    qi = pl.program_id(2)
    slope = sc_ref[h]
    lam = sc_ref[ATT_HEADS]
    nk = seq // tk
    n_diag = tk // tq
    rows = 2 * tq
    lane_tiles = tk // LANES

    @pl.when(qi == 0)
    def _():
        r = lax.broadcasted_iota(jnp.int32, (rows, tk), 0)
        c = lax.broadcasted_iota(jnp.int32, (rows, tk), 1)
        d0 = (jnp.where(r >= tq, r - tq, r) - c).astype(F32)
        tbl_ref[0] = slope * d0
        tbl_ref[1] = -slope * d0
        for d in range(n_diag):
            tbl_ref[2 + d] = slope * jnp.abs(d0 + float(d * tq))

    q = (q_ref[...].astype(F32) * (ATT_HEAD_DIM ** -0.5 * LOG2E)).astype(BF16)
    lane = lax.broadcasted_iota(jnp.int32, (tq, LANES), 1)
    zero = jnp.zeros_like(q)
    qs_ref[0:tq, :] = jnp.where(lane < ATT_HEAD_DIM, q, zero)
    qs_ref[tq:rows, :] = jnp.where(lane >= ATT_HEAD_DIM, q, zero)

    m_ref[...] = jnp.full(m_ref.shape, -1e30, F32)
    l_ref[...] = jnp.zeros(l_ref.shape, F32)
    acc_ref[...] = jnp.zeros(acc_ref.shape, F32)
    q0 = qi * tq
    kd = q0 // tk

    def tile_consts(kj):
        delta = (q0 - kj * tk).astype(F32)
        left, right = kj < kd, kj > kd
        sel = jnp.where(left, 0, jnp.where(right, 1, 2 + (q0 - kd * tk) // tq))
        cst = jnp.where(left, slope * delta, jnp.where(right, -slope * delta, 0.0))
        return sel, cst

    def scores(kj, slot):
        sel, _ = tile_consts(kj)
        kt = k_ref[pl.ds(pl.multiple_of(kj * tk, tk), tk), :]
        s = lax.dot_general(qs_ref[...], kt, (((1,), (1,)), ((), ())),
                            preferred_element_type=F32)
        pm = None
        for c in range(lane_tiles):
            cols = slice(c * LANES, (c + 1) * LANES)
            sb = s[:, cols] - tbl_ref[sel, :, cols]
            sb_ref[slot, :, cols] = sb
            pm = sb if pm is None else jnp.maximum(pm, sb)
        pm_ref[slot] = pm

    def accumulate(kj, slot):
        _, cst = tile_consts(kj)
        red = jnp.max(pm_ref[slot], axis=-1, keepdims=True) - cst
        m_old = m_ref[...]
        m_new = jnp.maximum(m_old, jnp.broadcast_to(red, m_old.shape))
        alpha = jnp.exp2(m_old - m_new)
        m_ref[...] = m_new
        shift = m_new + cst
        ps = [jnp.exp2(sb_ref[slot, :, c * LANES:(c + 1) * LANES] - shift).astype(BF16)
              for c in range(lane_tiles)]
        vt = v_ref[pl.ds(pl.multiple_of(kj * tk, tk), tk), :]
        v1 = jnp.concatenate([vt, jnp.ones_like(vt)], axis=1)
        pv = jnp.dot(jnp.concatenate(ps, axis=1), v1, preferred_element_type=F32)
        acc_ref[...] = alpha * acc_ref[...] + pv[:, :LANES]
        l_ref[...] = alpha * l_ref[...] + pv[:, LANES:]

    scores(0, 0)
    for kj in range(nk):
        if kj + 1 < nk:
            scores(kj + 1, (kj + 1) % 2)
        accumulate(kj, kj % 2)

    o = acc_ref[...] / l_ref[...]
    d = o[0:tq, :] - lam * o[tq:rows, :]
    o_ref[...] = (_rms(d, g_ref[...]) * (1.0 - LAMBDA_INIT)).astype(BF16)


def _attention(proj3, scal, g_subln, tq, tk):
    batch, seq, _ = proj3.shape
    kern = functools.partial(_attn_kernel, tq=tq, tk=tk, seq=seq)
    qb, kb, vb = COL_Q * HEAD_BLOCKS, COL_K * HEAD_BLOCKS, COL_VA * HEAD_BLOCKS
    return pl.pallas_call(
        kern,
        grid=(batch, ATT_HEADS, seq // tq),
        in_specs=[
            pl.BlockSpec(memory_space=pltpu.SMEM),
            pl.BlockSpec((None, tq, LANES), lambda b, h, i: (b, i, qb + h)),
            pl.BlockSpec((None, seq, LANES), lambda b, h, i: (b, 0, kb + h)),
            pl.BlockSpec((None, seq, LANES), lambda b, h, i: (b, 0, vb + h)),
            pl.BlockSpec((1, LANES), lambda b, h, i: (0, 0)),
        ],
        out_specs=pl.BlockSpec((None, tq, LANES), lambda b, h, i: (b, i, h)),
        out_shape=jax.ShapeDtypeStruct((batch, seq, D_MODEL), BF16),
        scratch_shapes=[
            pltpu.VMEM((2 * tq, LANES), BF16),
            pltpu.VMEM((2 + tk // tq, 2 * tq, tk), F32),
            pltpu.VMEM((2, 2 * tq, tk), F32),
            pltpu.VMEM((2, 2 * tq, LANES), F32),
            pltpu.VMEM((2 * tq, LANES), F32),
            pltpu.VMEM((2 * tq, LANES), F32),
            pltpu.VMEM((2 * tq, LANES), F32),
        ],
        compiler_params=pltpu.CompilerParams(
            dimension_semantics=("arbitrary", "arbitrary", "arbitrary"),
            vmem_limit_bytes=VMEM_LIMIT_BYTES),
        name="diffattn",
    )(scal, proj3, proj3, proj3, g_subln)


def _mix_kernel(u_ref, vn_ref, gs_ref, ga_ref, att_ref, x_ref, ws_ref, bs_ref,
                wbs_ref, wba_ref, wo_ref, gffn_ref, wr_ref, br_ref,
                x1_ref, hn_ref, lg_ref, sgu_ref, *, tm):
    for n in range(tm // CHUNK):
        rows = slice(n * CHUNK, (n + 1) * CHUNK)
        for g in range(SGU_GROUPS):
            cols = slice(g * CHUNK, (g + 1) * CHUNK)
            vm = jnp.dot(ws_ref[g], vn_ref[rows, cols], preferred_element_type=F32) + bs_ref[g]
            sgu_ref[rows, cols] = (u_ref[rows, cols].astype(F32) * vm).astype(BF16)

    for c in range(tm // MATMUL_ROWS):
        rows = slice(c * MATMUL_ROWS, (c + 1) * MATMUL_ROWS)
        a = jnp.dot(sgu_ref[rows, :], wbs_ref[...], preferred_element_type=F32)
        b = jnp.dot(att_ref[rows, :], wba_ref[...], preferred_element_type=F32)
        merged = gs_ref[rows, :].astype(F32) * a + ga_ref[rows, :].astype(F32) * b
        x1 = x_ref[rows, :] + jnp.dot(merged.astype(BF16), wo_ref[...],
                                      preferred_element_type=F32)
        x1_ref[rows, :] = x1
        hn = _rms(x1, gffn_ref[...]).astype(BF16)
        hn_ref[rows, :] = hn
        lg_ref[rows, :] = jnp.dot(hn, wr_ref[...], preferred_element_type=F32) + br_ref[...]


def _mix(proj, att, x2, ws, bs, wbs, wba, wo, g_ffn, wr, br, tm):
    tokens = x2.shape[0]
    row = lambda c: pl.BlockSpec((tm, D_MODEL), lambda i: (i, c))
    full = lambda shape: pl.BlockSpec(shape, lambda i: (0,) * len(shape))
    return pl.pallas_call(
        functools.partial(_mix_kernel, tm=tm),
        grid=(tokens // tm,),
        in_specs=[
            row(COL_U), row(COL_V), row(COL_GS), row(COL_GA),
            row(0), row(0),
            full((SGU_GROUPS, CHUNK, CHUNK)), full((SGU_GROUPS, CHUNK, CHUNK)),
            full((D_MODEL, D_MODEL)), full((D_MODEL, D_MODEL)), full((D_MODEL, D_MODEL)),
            full((1, D_MODEL)), full((D_MODEL, LANES)), full((1, LANES)),
        ],
        out_specs=[row(0), row(0), pl.BlockSpec((tm, LANES), lambda i: (i, 0))],
        out_shape=[
            jax.ShapeDtypeStruct((tokens, D_MODEL), F32),
            jax.ShapeDtypeStruct((tokens, D_MODEL), BF16),
            jax.ShapeDtypeStruct((tokens, LANES), F32),
        ],
        scratch_shapes=[pltpu.VMEM((tm, D_MODEL), BF16)],
        compiler_params=pltpu.CompilerParams(
            dimension_semantics=("arbitrary",),
            vmem_limit_bytes=VMEM_LIMIT_BYTES),
        name="mix",
    )(proj, proj, proj, proj, att, x2, ws, bs, wbs, wba, wo, g_ffn, wr, br)


def _run_copy(src_ref, src_row, dst_ref, dst_row, sem):
    return pltpu.make_async_copy(src_ref.at[pl.ds(src_row, RUN_CHUNK)],
                                 dst_ref.at[pl.ds(dst_row, RUN_CHUNK)], sem)


def _dispatch_kernel(bv_ref, meta_ref, post_ref, *rest, bm, tile_starts, n_tiles):
    hn_refs = rest[:len(tile_starts)]
    xs_ref, srt_ref, zero_ref, pending_ref, sem = rest[len(tile_starts):]
    i = pl.program_id(0)

    @pl.when(i == 0)
    def _():
        zero_ref[...] = jnp.zeros(zero_ref.shape, U32)

        def maybe_zero(b, carry):
            @pl.when(bv_ref[b] < bm)
            def _():
                cp = pltpu.make_async_copy(
                    zero_ref, xs_ref.at[pl.ds(pl.multiple_of(b * bm, bm), bm)], sem)
                cp.start()
                cp.wait()
            return carry

        lax.fori_loop(0, xs_ref.shape[0] // bm, maybe_zero, 0)

    hn = hn_refs[0][...]
    for ref, start in zip(hn_refs[1:], tile_starts[1:]):
        hn = jnp.where(i >= start, ref[...], hn)

    slot = lax.broadcasted_iota(I32, (SORTED_ROWS, TOKEN_TILE), 0)
    hit = slot == post_ref[0:1, :]
    for k in range(1, TOP_K):
        hit = jnp.logical_or(hit, slot == post_ref[k:k + 1, :])
    onehot = jnp.where(hit, 1.0, 0.0).astype(BF16)
    buf = srt_ref.at[i % 2]
    buf[...] = _pack_rows(jnp.dot(onehot, hn, preferred_element_type=F32))

    def drain(count):
        def wait_one(j, carry):
            _run_copy(buf, 0, xs_ref, 0, sem).wait()
            return carry
        lax.fori_loop(0, count, wait_one, 0)

    @pl.when(i > 0)
    def _():
        drain(pending_ref[0])

    n_copies = meta_ref[0, 0]

    def issue(j, carry):
        src = pl.multiple_of(meta_ref[0, 1 + j], RUN_CHUNK)
        dst = pl.multiple_of(meta_ref[0, 1 + MAX_COPIES + j], ROW_ALIGN)
        _run_copy(buf, src, xs_ref, dst, sem).start()
        return carry

    lax.fori_loop(0, n_copies, issue, 0)
    pending_ref[0] = n_copies

    @pl.when(i == n_tiles - 1)
    def _():
        drain(n_copies)


def _dispatch(hns, meta, post, block_valid, n_rows_padded, bm):
    tiles = [hn.shape[0] // TOKEN_TILE for hn in hns]
    tile_starts = tuple(sum(tiles[:k]) for k in range(len(tiles)))

    def hn_spec(start, count):
        return pl.BlockSpec((TOKEN_TILE, D_MODEL),
                            lambda i, bv: (jnp.clip(i - start, 0, count - 1), 0))

    in_specs = [
        pl.BlockSpec((None, 1, META_WIDTH), lambda i, bv: (i, 0, 0), memory_space=pltpu.SMEM),
        pl.BlockSpec((None, TOP_K, TOKEN_TILE), lambda i, bv: (i, 0, 0)),
    ] + [hn_spec(s, c) for s, c in zip(tile_starts, tiles)]
    grid_spec = pltpu.PrefetchScalarGridSpec(
        num_scalar_prefetch=1,
        grid=(sum(tiles),),
        in_specs=in_specs,
        out_specs=pl.BlockSpec(memory_space=pl.ANY),
        scratch_shapes=[pltpu.VMEM((2, SORTED_ROWS, PACKED), U32),
                        pltpu.VMEM((bm, PACKED), U32),
                        pltpu.SMEM((1,), I32),
                        pltpu.SemaphoreType.DMA],
    )
    return pl.pallas_call(
        functools.partial(_dispatch_kernel, bm=bm, tile_starts=tile_starts, n_tiles=sum(tiles)),
        grid_spec=grid_spec,
        out_shape=jax.ShapeDtypeStruct((n_rows_padded, PACKED), U32),
        compiler_params=pltpu.CompilerParams(
            dimension_semantics=("arbitrary",),
            vmem_limit_bytes=VMEM_LIMIT_BYTES),
        name="dispatch",
    )(block_valid, meta, post, *hns)


def _expert_kernel(be_ref, bv_ref, xs_ref, wgu_ref, bgu_ref, wd_ref, bd_ref, y_ref,
                   wgu_bf_ref, wd_bf_ref, *, bm):
    b = pl.program_id(0)
    valid = bv_ref[b]

    fresh = jnp.logical_or(b == 0, be_ref[b] != be_ref[jnp.maximum(b - 1, 0)])

    @pl.when(jnp.logical_and(valid > 0, fresh))
    def _():
        wgu_bf_ref[...] = wgu_ref[...].astype(BF16)
        wd_bf_ref[...] = wd_ref[...].astype(BF16)

    @pl.when(valid > 0)
    def _():
        row = lax.broadcasted_iota(I32, (bm, PACKED), 0)
        w = xs_ref[...]
        lo, hi = _unpack_rows(jnp.where(row < valid, w, jnp.zeros_like(w)))
        x = jnp.concatenate([lo, hi], axis=1)
        gu = jnp.dot(x, wgu_bf_ref[...], preferred_element_type=F32) + bgu_ref[...]
        glu = jnp.minimum(gu[:, :D_FF], SWIGLU_LIMIT)
        lin = jnp.clip(gu[:, D_FF:], -SWIGLU_LIMIT, SWIGLU_LIMIT)
        act = glu * jax.nn.sigmoid(SWIGLU_ALPHA * glu) * (lin + 1.0)
        y = jnp.dot(act.astype(BF16), wd_bf_ref[...], preferred_element_type=F32) + bd_ref[...]
        y_ref[...] = _pack_rows(y.astype(BF16).astype(F32))

    @pl.when(valid <= 0)
    def _():
        y_ref[...] = jnp.zeros(y_ref.shape, U32)


def _experts(xs, block_expert, block_valid, wgu, bgu, wd, bd, bm):
    n_blocks = xs.shape[0] // bm
    grid_spec = pltpu.PrefetchScalarGridSpec(
        num_scalar_prefetch=2,
        grid=(n_blocks,),
        in_specs=[
            pl.BlockSpec((bm, PACKED), lambda b, be, bv: (b, 0)),
            pl.BlockSpec((None, D_MODEL, 2 * D_FF), lambda b, be, bv: (be[b], 0, 0)),
            pl.BlockSpec((None, 1, 2 * D_FF), lambda b, be, bv: (be[b], 0, 0)),
            pl.BlockSpec((None, D_FF, D_MODEL), lambda b, be, bv: (be[b], 0, 0)),
            pl.BlockSpec((None, 1, D_MODEL), lambda b, be, bv: (be[b], 0, 0)),
        ],
        out_specs=pl.BlockSpec((bm, PACKED), lambda b, be, bv: (b, 0)),
        scratch_shapes=[pltpu.VMEM((D_MODEL, 2 * D_FF), BF16), pltpu.VMEM((D_FF, D_MODEL), BF16)],
    )
    return pl.pallas_call(
        functools.partial(_expert_kernel, bm=bm),
        grid_spec=grid_spec,
        out_shape=jax.ShapeDtypeStruct(xs.shape, U32),
        compiler_params=pltpu.CompilerParams(
            dimension_semantics=("arbitrary",),
            vmem_limit_bytes=VMEM_LIMIT_BYTES),
        name="experts",
    )(block_expert, block_valid, xs, wgu, bgu, wd, bd)


def _combine_kernel(meta_ref, next_meta_ref, pos_ref, gate_ref, x1_ref, gfin_ref, y_ref, o_ref,
                    buf_ref, sems, *, n_tiles):
    i = pl.program_id(0)
    cur = i % 2

    def fetch(m_ref, which):
        def issue(j, carry):
            dst = pl.multiple_of(m_ref[0, 1 + j], RUN_CHUNK)
            src = pl.multiple_of(m_ref[0, 1 + MAX_COPIES + j], ROW_ALIGN)
            _run_copy(y_ref, src, buf_ref.at[which], dst, sems.at[which]).start()
            return carry
        lax.fori_loop(0, m_ref[0, 0], issue, 0)

    @pl.when(i == 0)
    def _():
        buf_ref[...] = jnp.zeros(buf_ref.shape, U32)
        fetch(meta_ref, 0)

    @pl.when(i + 1 < n_tiles)
    def _():
        fetch(next_meta_ref, 1 - cur)

    def drain(j, carry):
        _run_copy(y_ref, 0, buf_ref.at[cur], 0, sems.at[cur]).wait()
        return carry

    lax.fori_loop(0, meta_ref[0, 0], drain, 0)

    slot = lax.broadcasted_iota(I32, (TOKEN_TILE, SORTED_ROWS), 1)
    pos = pos_ref[...]
    gates = gate_ref[...]
    g = jnp.zeros((TOKEN_TILE, SORTED_ROWS), F32)
    for k in range(TOP_K):
        g = jnp.where(slot == pos[:, k:k + 1], gates[:, k:k + 1], g)
    g = g.astype(BF16)
    lo, hi = _unpack_rows(buf_ref[cur])
    moe = jnp.concatenate([jnp.dot(g, lo, preferred_element_type=F32),
                           jnp.dot(g, hi, preferred_element_type=F32)], axis=1)
    o_ref[...] = _rms(x1_ref[...] + moe, gfin_ref[...])


def _combine(yrows, meta, pos, gates, x1, g_final):
    tokens = x1.shape[0]
    n_tiles = tokens // TOKEN_TILE
    meta_spec = lambda ahead: pl.BlockSpec(
        (None, 1, META_WIDTH), lambda i: (jnp.minimum(i + ahead, n_tiles - 1), 0, 0),
        memory_space=pltpu.SMEM)
    grid_spec = pltpu.PrefetchScalarGridSpec(
        num_scalar_prefetch=0,
        grid=(n_tiles,),
        in_specs=[
            meta_spec(0), meta_spec(1),
            pl.BlockSpec((TOKEN_TILE, TOP_K), lambda i: (i, 0)),
            pl.BlockSpec((TOKEN_TILE, TOP_K), lambda i: (i, 0)),
            pl.BlockSpec((TOKEN_TILE, D_MODEL), lambda i: (i, 0)),
            pl.BlockSpec((1, D_MODEL), lambda i: (0, 0)),
            pl.BlockSpec(memory_space=pl.ANY),
        ],
        out_specs=pl.BlockSpec((TOKEN_TILE, D_MODEL), lambda i: (i, 0)),
        scratch_shapes=[pltpu.VMEM((2, SORTED_ROWS, PACKED), U32),
                        pltpu.SemaphoreType.DMA((2,))],
    )
    return pl.pallas_call(
        functools.partial(_combine_kernel, n_tiles=n_tiles),
        grid_spec=grid_spec,
        out_shape=jax.ShapeDtypeStruct((tokens, D_MODEL), F32),
        compiler_params=pltpu.CompilerParams(
            dimension_semantics=("arbitrary",),
            vmem_limit_bytes=VMEM_LIMIT_BYTES),
        name="combine",
    )(meta, meta, pos, gates, x1, g_final, yrows)


def _route(logits, bm):
    tokens = logits.shape[0]
    n_tiles = tokens // TOKEN_TILE
    spare = RUN_CHUNK + bm - 1
    max_rows = tokens * TOP_K + n_tiles * N_EXPERTS * (ROW_ALIGN - 1) + N_EXPERTS * spare
    n_blocks = -(-max_rows // bm)
    top_val, top_idx = lax.top_k(logits, TOP_K)
    gates = jax.nn.softmax(top_val, axis=-1)
    onehot = top_idx[:, :, None] == jnp.arange(N_EXPERTS, dtype=top_idx.dtype)
    sel = jnp.sum(onehot, axis=1, dtype=F32).reshape(n_tiles, TOKEN_TILE, N_EXPERTS)
    earlier = jnp.tril(jnp.ones((TOKEN_TILE, TOKEN_TILE), F32), -1)
    rank = jnp.einsum('ts,nse->nte', earlier, sel).astype(I32)
    runs = jnp.sum(sel, axis=1).astype(I32)
    runs_aligned = (runs + ROW_ALIGN - 1) // ROW_ALIGN * ROW_ALIGN
    base = jnp.cumsum(runs_aligned, axis=0) - runs_aligned
    counts = jnp.sum(runs_aligned, axis=0)
    padded_counts = (counts + spare) // bm * bm
    padded_end = jnp.cumsum(padded_counts)
    padded_start = padded_end - padded_counts
    runs_padded = (runs + RUN_CHUNK - 1) // RUN_CHUNK * RUN_CHUNK
    seg = jnp.cumsum(runs_padded, axis=1) - runs_padded
    slot_of = (seg[:, None, :] + rank).reshape(tokens, 1, N_EXPERTS)
    pos = jnp.sum(jnp.where(onehot, slot_of, 0), axis=-1).astype(I32)

    n_chunks = runs_padded // RUN_CHUNK
    chunk_end = jnp.cumsum(n_chunks, axis=1)
    j = jnp.arange(MAX_COPIES, dtype=I32)
    chunk_start = chunk_end - n_chunks
    owns = jnp.logical_and(j[None, :, None] >= chunk_start[:, None, :],
                           j[None, :, None] < chunk_end[:, None, :])
    pick = lambda a: jnp.sum(jnp.where(owns, a[:, None, :], 0), axis=-1)
    within = (j[None, :] - pick(chunk_start)) * RUN_CHUNK
    live = j[None, :] < chunk_end[:, -1:]
    src = jnp.where(live, pick(seg) + within, 0)
    dst = jnp.where(live, pick(padded_start[None, :] + base) + within, 0)
    meta = jnp.concatenate([chunk_end[:, -1:], src, dst], axis=1).astype(I32)
    meta = jnp.pad(meta, ((0, 0), (0, META_WIDTH - meta.shape[1]))).reshape(n_tiles, 1, META_WIDTH)

    block_start = jnp.arange(n_blocks, dtype=I32) * bm
    in_region = jnp.logical_and(block_start[:, None] >= padded_start[None, :],
                                block_start[:, None] < padded_end[None, :])
    expert_ids = jnp.arange(N_EXPERTS, dtype=I32)
    block_expert = jnp.where(jnp.any(in_region, axis=1),
                             jnp.sum(jnp.where(in_region, expert_ids[None, :], 0), axis=1),
                             N_EXPERTS - 1).astype(I32)
    rows_left = jnp.sum(jnp.where(in_region, (padded_start + counts)[None, :], 0), axis=1) - block_start
    block_valid = jnp.clip(rows_left, 0, bm)
    post = pos.reshape(n_tiles, TOKEN_TILE, TOP_K).transpose(0, 2, 1)
    return gates, pos, post, meta, block_expert, block_valid.astype(I32), n_blocks


def _token_mixer(x, p):
    batch, seq, _ = x.shape
    t = _tiles(batch, seq)
    x2 = x.reshape(batch * seq, D_MODEL)
    proj = _inproj(x2, p["g_mix"], p["w_in"], p["g_sgu"], t["tm_in"])
    att = _attention(proj.reshape(batch, seq, IN_COLS), p["attn_scalars"], p["g_subln"],
                     t["tq"], t["tk"])
    return _mix(proj, att.reshape(batch * seq, D_MODEL), x2, p["w_spatial"], p["b_spatial"],
                p["w_branch_sgu"], p["w_branch_attn"], p["w_out"], p["g_ffn"],
                p["w_router"], p["b_router"], t["tm_mix"])


def _moe(mixed, p):
    bm = ROUTE_BLOCK
    logits = jnp.concatenate([lg[:, :N_EXPERTS] for _, _, lg in mixed], axis=0)
    gates, pos, post, meta, block_expert, block_valid, n_blocks = _route(logits, bm)
    xs = _dispatch([hn for _, hn, _ in mixed], meta, post, block_valid, n_blocks * bm, bm)
    tile0, spans = 0, []
    for x1, _, _ in mixed:
        spans.append((tile0, tile0 + x1.shape[0] // TOKEN_TILE))
        tile0 = spans[-1][1]
    yrows = _experts(xs, block_expert, block_valid, p["w_gate_up"], p["b_gate_up"],
                     p["w_down"], p["b_down"], bm)
    outs = []
    for (x1, _, _), (t0, t1) in zip(mixed, spans):
        tok = slice(t0 * TOKEN_TILE, t1 * TOKEN_TILE)
        outs.append(_combine(yrows, meta[t0:t1], pos[tok], gates[tok], x1, p["g_final"]))
    return outs


def _prepare_params(g_mix, w_in, g_sgu, w_spatial, b_spatial, lambda_q1, lambda_k1, lambda_q2,
                    lambda_k2, g_subln, w_branch_sgu, w_branch_attn, w_out, g_ffn, w_router,
                    b_router, w_gate_up, b_gate_up, w_down, b_down, g_final):
    lam = (jnp.exp(jnp.sum(lambda_q1[0].astype(F32) * lambda_k1[0].astype(F32)))
           - jnp.exp(jnp.sum(lambda_q2[0].astype(F32) * lambda_k2[0].astype(F32)))
           + LAMBDA_INIT)
    slopes = jnp.exp2(-8.0 * jnp.arange(1, ATT_HEADS + 1, dtype=F32) / ATT_HEADS) * LOG2E
    pad = LANES - N_EXPERTS
    return dict(
        g_mix=g_mix[0].reshape(1, D_MODEL),
        w_in=w_in[0].astype(BF16),
        g_sgu=g_sgu[0].reshape(1, D_MODEL),
        w_spatial=w_spatial[0].astype(BF16),
        b_spatial=jnp.broadcast_to(b_spatial[0][:, :, None], (SGU_GROUPS, CHUNK, CHUNK)).astype(F32),
        attn_scalars=jnp.concatenate([slopes, lam.reshape(1)]).astype(F32),
        g_subln=g_subln[0].reshape(1, ATT_VDIM),
        w_branch_sgu=w_branch_sgu[0].astype(BF16),
        w_branch_attn=w_branch_attn[0].astype(BF16),
        w_out=w_out[0].astype(BF16),
        g_ffn=g_ffn[0].reshape(1, D_MODEL),
        w_router=jnp.pad(w_router[0], ((0, 0), (0, pad))).astype(BF16),
        b_router=jnp.pad(b_router[0], (0, pad)).reshape(1, LANES).astype(F32),
        w_gate_up=w_gate_up[0],
        b_gate_up=b_gate_up[0].reshape(N_EXPERTS, 1, 2 * D_FF),
        w_down=w_down[0],
        b_down=b_down[0].reshape(N_EXPERTS, 1, D_MODEL),
        g_final=g_final.reshape(1, D_MODEL),
    )


def _encode(xs, p):
    outs = _moe([_token_mixer(x, p) for x in xs], p)
    return tuple(o.reshape(x.shape) for o, x in zip(outs, xs))


def kernel(x_prompt, x_sample, g_mix, w_in, g_sgu, w_spatial, b_spatial, lambda_q1, lambda_k1, lambda_q2, lambda_k2, g_subln, w_branch_sgu, w_branch_attn, w_out, g_ffn, w_router, b_router, w_gate_up, b_gate_up, w_down, b_down, g_final):
    p = _prepare_params(g_mix, w_in, g_sgu, w_spatial, b_spatial, lambda_q1, lambda_k1, lambda_q2,
                        lambda_k2, g_subln, w_branch_sgu, w_branch_attn, w_out, g_ffn, w_router,
                        b_router, w_gate_up, b_gate_up, w_down, b_down, g_final)
    return _encode((x_prompt, x_sample), p)
```

```python
import functools
import math

import jax
import jax.numpy as jnp
from jax import lax
from jax.experimental import pallas as pl
from jax.experimental.pallas import tpu as pltpu

D_MODEL = 1024
SGU_GROUPS = 8
CHUNK = 128
ATT_HEADS = 8
ATT_HEAD_DIM = 64
ATT_VDIM = 2 * ATT_HEAD_DIM
N_EXPERTS = 32
TOP_K = 4
D_FF = D_MODEL
SWIGLU_LIMIT = 7.0
SWIGLU_ALPHA = 1.702
RMS_EPS = 1e-5
LAMBDA_INIT = 0.8 - 0.6 * math.exp(-0.3 * 0)
LOG2E = 1.4426950408889634

COL_U, COL_V, COL_Q, COL_K, COL_VA, COL_GS, COL_GA = range(7)
IN_COLS = 7 * D_MODEL
LANES = 128
HEAD_BLOCKS = D_MODEL // LANES

VMEM_LIMIT_BYTES = 56 * 1024 * 1024

F32 = jnp.float32
BF16 = jnp.bfloat16
U32 = jnp.uint32
I32 = jnp.int32


def _tiles(batch, seq):
    tokens = batch * seq
    return dict(
        tm_in=min(1024, tokens),
        tq=min(256, seq),
        tk=min(512, seq),
        tm_mix=min(512, tokens),
    )


MATMUL_ROWS = 256
ROUTE_BLOCK = 512
TOKEN_TILE = 256
RUN_CHUNK = 16
ROW_ALIGN = 8
SORTED_ROWS = -(-(TOKEN_TILE * TOP_K + N_EXPERTS * (RUN_CHUNK - 1)) // LANES) * LANES
MAX_COPIES = SORTED_ROWS // RUN_CHUNK
META_WIDTH = -(-(1 + 2 * MAX_COPIES) // LANES) * LANES
PACKED = D_MODEL // 2
HI_MASK = 0xFFFF0000


def _gelu_tanh(x):
    return 0.5 * x * (1.0 + jnp.tanh(math.sqrt(2.0 / math.pi) * (x + 0.044715 * (x * x * x))))


def _rms(x, g):
    r = x * lax.rsqrt(jnp.mean(x * x, axis=-1, keepdims=True) + RMS_EPS)
    return r * g


def _pack_rows(x):
    lo = lax.bitcast_convert_type(x[:, :PACKED], U32) >> jnp.uint32(16)
    hi = lax.bitcast_convert_type(x[:, PACKED:], U32) & jnp.uint32(HI_MASK)
    return lo | hi


def _unpack_rows(w):
    lo = lax.bitcast_convert_type(w << jnp.uint32(16), F32).astype(BF16)
    hi = lax.bitcast_convert_type(w & jnp.uint32(HI_MASK), F32).astype(BF16)
    return lo, hi


def _inproj_kernel(x_ref, gmix_ref, w_ref, gsgu_ref, o_ref, xn_ref, *, tm):
    j = pl.program_id(1)

    @pl.when(j == 0)
    def _():
        xn_ref[...] = _rms(x_ref[...], gmix_ref[...]).astype(BF16)

    def project(epilogue):
        for c in range(tm // MATMUL_ROWS):
            rows = slice(c * MATMUL_ROWS, (c + 1) * MATMUL_ROWS)
            y = jnp.dot(xn_ref[rows, :], w_ref[...], preferred_element_type=F32)
            o_ref[rows, :] = epilogue(y).astype(BF16)

    @pl.when(j == COL_U)
    def _():
        project(_gelu_tanh)

    @pl.when(j == COL_V)
    def _():
        project(lambda y: _rms(_gelu_tanh(y), gsgu_ref[...]))

    @pl.when(jnp.logical_and(j >= COL_Q, j <= COL_VA))
    def _():
        project(lambda y: y)

    @pl.when(j >= COL_GS)
    def _():
        project(jax.nn.sigmoid)


def _inproj(x2, g_mix, w_in, g_sgu, tm):
    tokens = x2.shape[0]
    return pl.pallas_call(
        functools.partial(_inproj_kernel, tm=tm),
        grid=(tokens // tm, IN_COLS // D_MODEL),
        in_specs=[
            pl.BlockSpec((tm, D_MODEL), lambda i, j: (i, 0)),
            pl.BlockSpec((1, D_MODEL), lambda i, j: (0, 0)),
            pl.BlockSpec((D_MODEL, D_MODEL), lambda i, j: (0, j)),
            pl.BlockSpec((1, D_MODEL), lambda i, j: (0, 0)),
        ],
        out_specs=pl.BlockSpec((tm, D_MODEL), lambda i, j: (i, j)),
        out_shape=jax.ShapeDtypeStruct((tokens, IN_COLS), BF16),
        scratch_shapes=[pltpu.VMEM((tm, D_MODEL), BF16)],
        compiler_params=pltpu.CompilerParams(
            dimension_semantics=("arbitrary", "arbitrary"),
            vmem_limit_bytes=VMEM_LIMIT_BYTES),
        name="inproj",
    )(x2, g_mix, w_in, g_sgu)


def _attn_kernel(sc_ref, q_ref, k_ref, v_ref, g_ref, o_ref,
                 qs_ref, tbl_ref, sb_ref, pm_ref, m_ref, l_ref, acc_ref, *, tq, tk, seq):
    h = pl.program_id(1)
    qi = pl.program_id(2)
    slope = sc_ref[h]
    lam = sc_ref[ATT_HEADS]
    nk = seq // tk
    n_diag = tk // tq
    rows = 2 * tq
    lane_tiles = tk // LANES

    @pl.when(qi == 0)
    def _():
        r = lax.broadcasted_iota(jnp.int32, (rows, tk), 0)
        c = lax.broadcasted_iota(jnp.int32, (rows, tk), 1)
        d0 = (jnp.where(r >= tq, r - tq, r) - c).astype(F32)
        tbl_ref[0] = slope * d0
        tbl_ref[1] = -slope * d0
        for d in range(n_diag):
            tbl_ref[2 + d] = slope * jnp.abs(d0 + float(d * tq))

    q = (q_ref[...].astype(F32) * (ATT_HEAD_DIM ** -0.5 * LOG2E)).astype(BF16)
    lane = lax.broadcasted_iota(jnp.int32, (tq, LANES), 1)
    zero = jnp.zeros_like(q)
    qs_ref[0:tq, :] = jnp.where(lane < ATT_HEAD_DIM, q, zero)
    qs_ref[tq:rows, :] = jnp.where(lane >= ATT_HEAD_DIM, q, zero)

    m_ref[...] = jnp.full(m_ref.shape, -1e30, F32)
    l_ref[...] = jnp.zeros(l_ref.shape, F32)
    acc_ref[...] = jnp.zeros(acc_ref.shape, F32)
    q0 = qi * tq
    kd = q0 // tk

    def tile_consts(kj):
        delta = (q0 - kj * tk).astype(F32)
        left, right = kj < kd, kj > kd
        sel = jnp.where(left, 0, jnp.where(right, 1, 2 + (q0 - kd * tk) // tq))
        cst = jnp.where(left, slope * delta, jnp.where(right, -slope * delta, 0.0))
        return sel, cst

    def scores(kj, slot):
        sel, _ = tile_consts(kj)
        kt = k_ref[pl.ds(pl.multiple_of(kj * tk, tk), tk), :]
        s = lax.dot_general(qs_ref[...], kt, (((1,), (1,)), ((), ())),
                            preferred_element_type=F32)
        pm = None
        for c in range(lane_tiles):
            cols = slice(c * LANES, (c + 1) * LANES)
            sb = s[:, cols] - tbl_ref[sel, :, cols]
            sb_ref[slot, :, cols] = sb
            pm = sb if pm is None else jnp.maximum(pm, sb)
        pm_ref[slot] = pm

    def accumulate(kj, slot):
        _, cst = tile_consts(kj)
        red = jnp.max(pm_ref[slot], axis=-1, keepdims=True) - cst
        m_old = m_ref[...]
        m_new = jnp.maximum(m_old, jnp.broadcast_to(red, m_old.shape))
        alpha = jnp.exp2(m_old - m_new)
        m_ref[...] = m_new
        shift = m_new + cst
        ps = [jnp.exp2(sb_ref[slot, :, c * LANES:(c + 1) * LANES] - shift).astype(BF16)
              for c in range(lane_tiles)]
        vt = v_ref[pl.ds(pl.multiple_of(kj * tk, tk), tk), :]
        v1 = jnp.concatenate([vt, jnp.ones_like(vt)], axis=1)
        pv = jnp.dot(jnp.concatenate(ps, axis=1), v1, preferred_element_type=F32)
        acc_ref[...] = alpha * acc_ref[...] + pv[:, :LANES]
        l_ref[...] = alpha * l_ref[...] + pv[:, LANES:]

    scores(0, 0)
    for kj in range(nk):
        if kj + 1 < nk:
            scores(kj + 1, (kj + 1) % 2)
        accumulate(kj, kj % 2)

    o = acc_ref[...] / l_ref[...]
    d = o[0:tq, :] - lam * o[tq:rows, :]
    o_ref[...] = (_rms(d, g_ref[...]) * (1.0 - LAMBDA_INIT)).astype(BF16)


def _attention(proj3, scal, g_subln, tq, tk):
    batch, seq, _ = proj3.shape
    kern = functools.partial(_attn_kernel, tq=tq, tk=tk, seq=seq)
    qb, kb, vb = COL_Q * HEAD_BLOCKS, COL_K * HEAD_BLOCKS, COL_VA * HEAD_BLOCKS
    return pl.pallas_call(
        kern,
        grid=(batch, ATT_HEADS, seq // tq),
        in_specs=[
            pl.BlockSpec(memory_space=pltpu.SMEM),
            pl.BlockSpec((None, tq, LANES), lambda b, h, i: (b, i, qb + h)),
            pl.BlockSpec((None, seq, LANES), lambda b, h, i: (b, 0, kb + h)),
            pl.BlockSpec((None, seq, LANES), lambda b, h, i: (b, 0, vb + h)),
            pl.BlockSpec((1, LANES), lambda b, h, i: (0, 0)),
        ],
        out_specs=pl.BlockSpec((None, tq, LANES), lambda b, h, i: (b, i, h)),
        out_shape=jax.ShapeDtypeStruct((batch, seq, D_MODEL), BF16),
        scratch_shapes=[
            pltpu.VMEM((2 * tq, LANES), BF16),
            pltpu.VMEM((2 + tk // tq, 2 * tq, tk), F32),
            pltpu.VMEM((2, 2 * tq, tk), F32),
            pltpu.VMEM((2, 2 * tq, LANES), F32),
            pltpu.VMEM((2 * tq, LANES), F32),
            pltpu.VMEM((2 * tq, LANES), F32),
            pltpu.VMEM((2 * tq, LANES), F32),
        ],
        compiler_params=pltpu.CompilerParams(
            dimension_semantics=("arbitrary", "arbitrary", "arbitrary"),
            vmem_limit_bytes=VMEM_LIMIT_BYTES),
        name="diffattn",
    )(scal, proj3, proj3, proj3, g_subln)


def _mix_kernel(u_ref, vn_ref, gs_ref, ga_ref, att_ref, x_ref, ws_ref, bs_ref,
                wbs_ref, wba_ref, wo_ref, gffn_ref, wr_ref, br_ref,
                x1_ref, hn_ref, lg_ref, sgu_ref, *, tm):
    for n in range(tm // CHUNK):
        rows = slice(n * CHUNK, (n + 1) * CHUNK)
        for g in range(SGU_GROUPS):
            cols = slice(g * CHUNK, (g + 1) * CHUNK)
            vm = jnp.dot(ws_ref[g], vn_ref[rows, cols], preferred_element_type=F32) + bs_ref[g]
            sgu_ref[rows, cols] = (u_ref[rows, cols].astype(F32) * vm).astype(BF16)

    for c in range(tm // MATMUL_ROWS):
        rows = slice(c * MATMUL_ROWS, (c + 1) * MATMUL_ROWS)
        a = jnp.dot(sgu_ref[rows, :], wbs_ref[...], preferred_element_type=F32)
        b = jnp.dot(att_ref[rows, :], wba_ref[...], preferred_element_type=F32)
        merged = gs_ref[rows, :].astype(F32) * a + ga_ref[rows, :].astype(F32) * b
        x1 = x_ref[rows, :] + jnp.dot(merged.astype(BF16), wo_ref[...],
                                      preferred_element_type=F32)
        x1_ref[rows, :] = x1
        hn = _rms(x1, gffn_ref[...]).astype(BF16)
        hn_ref[rows, :] = hn
        lg_ref[rows, :] = jnp.dot(hn, wr_ref[...], preferred_element_type=F32) + br_ref[...]


def _mix(proj, att, x2, ws, bs, wbs, wba, wo, g_ffn, wr, br, tm):
    tokens = x2.shape[0]
    row = lambda c: pl.BlockSpec((tm, D_MODEL), lambda i: (i, c))
    full = lambda shape: pl.BlockSpec(shape, lambda i: (0,) * len(shape))
    return pl.pallas_call(
        functools.partial(_mix_kernel, tm=tm),
        grid=(tokens // tm,),
        in_specs=[
            row(COL_U), row(COL_V), row(COL_GS), row(COL_GA),
            row(0), row(0),
            full((SGU_GROUPS, CHUNK, CHUNK)), full((SGU_GROUPS, CHUNK, CHUNK)),
            full((D_MODEL, D_MODEL)), full((D_MODEL, D_MODEL)), full((D_MODEL, D_MODEL)),
            full((1, D_MODEL)), full((D_MODEL, LANES)), full((1, LANES)),
        ],
        out_specs=[row(0), row(0), pl.BlockSpec((tm, LANES), lambda i: (i, 0))],
        out_shape=[
            jax.ShapeDtypeStruct((tokens, D_MODEL), F32),
            jax.ShapeDtypeStruct((tokens, D_MODEL), BF16),
            jax.ShapeDtypeStruct((tokens, LANES), F32),
        ],
        scratch_shapes=[pltpu.VMEM((tm, D_MODEL), BF16)],
        compiler_params=pltpu.CompilerParams(
            dimension_semantics=("arbitrary",),
            vmem_limit_bytes=VMEM_LIMIT_BYTES),
        name="mix",
    )(proj, proj, proj, proj, att, x2, ws, bs, wbs, wba, wo, g_ffn, wr, br)


def _run_copy(src_ref, src_row, dst_ref, dst_row, sem):
    return pltpu.make_async_copy(src_ref.at[pl.ds(src_row, RUN_CHUNK)],
                                 dst_ref.at[pl.ds(dst_row, RUN_CHUNK)], sem)


def _dispatch_kernel(bv_ref, meta_ref, post_ref, *rest, bm, tile_starts, n_tiles):
    hn_refs = rest[:len(tile_starts)]
    xs_ref, srt_ref, zero_ref, pending_ref, sem = rest[len(tile_starts):]
    i = pl.program_id(0)

    @pl.when(i == 0)
    def _():
        zero_ref[...] = jnp.zeros(zero_ref.shape, U32)

        def maybe_zero(b, carry):
            @pl.when(bv_ref[b] < bm)
            def _():
                cp = pltpu.make_async_copy(
                    zero_ref, xs_ref.at[pl.ds(pl.multiple_of(b * bm, bm), bm)], sem)
                cp.start()
                cp.wait()
            return carry

        lax.fori_loop(0, xs_ref.shape[0] // bm, maybe_zero, 0)

    hn = hn_refs[0][...]
    for ref, start in zip(hn_refs[1:], tile_starts[1:]):
        hn = jnp.where(i >= start, ref[...], hn)

    slot = lax.broadcasted_iota(I32, (SORTED_ROWS, TOKEN_TILE), 0)
    hit = slot == post_ref[0:1, :]
    for k in range(1, TOP_K):
        hit = jnp.logical_or(hit, slot == post_ref[k:k + 1, :])
    onehot = jnp.where(hit, 1.0, 0.0).astype(BF16)
    buf = srt_ref.at[i % 2]
    buf[...] = _pack_rows(jnp.dot(onehot, hn, preferred_element_type=F32))

    def drain(count):
        def wait_one(j, carry):
            _run_copy(buf, 0, xs_ref, 0, sem).wait()
            return carry
        lax.fori_loop(0, count, wait_one, 0)

    @pl.when(i > 0)
    def _():
        drain(pending_ref[0])

    n_copies = meta_ref[0, 0]

    def issue(j, carry):
        src = pl.multiple_of(meta_ref[0, 1 + j], RUN_CHUNK)
        dst = pl.multiple_of(meta_ref[0, 1 + MAX_COPIES + j], ROW_ALIGN)
        _run_copy(buf, src, xs_ref, dst, sem).start()
        return carry

    lax.fori_loop(0, n_copies, issue, 0)
    pending_ref[0] = n_copies

    @pl.when(i == n_tiles - 1)
    def _():
        drain(n_copies)


def _dispatch(hns, meta, post, block_valid, n_rows_padded, bm):
    tiles = [hn.shape[0] // TOKEN_TILE for hn in hns]
    tile_starts = tuple(sum(tiles[:k]) for k in range(len(tiles)))

    def hn_spec(start, count):
        return pl.BlockSpec((TOKEN_TILE, D_MODEL),
                            lambda i, bv: (jnp.clip(i - start, 0, count - 1), 0))

    in_specs = [
        pl.BlockSpec((None, 1, META_WIDTH), lambda i, bv: (i, 0, 0), memory_space=pltpu.SMEM),
        pl.BlockSpec((None, TOP_K, TOKEN_TILE), lambda i, bv: (i, 0, 0)),
    ] + [hn_spec(s, c) for s, c in zip(tile_starts, tiles)]
    grid_spec = pltpu.PrefetchScalarGridSpec(
        num_scalar_prefetch=1,
        grid=(sum(tiles),),
        in_specs=in_specs,
        out_specs=pl.BlockSpec(memory_space=pl.ANY),
        scratch_shapes=[pltpu.VMEM((2, SORTED_ROWS, PACKED), U32),
                        pltpu.VMEM((bm, PACKED), U32),
                        pltpu.SMEM((1,), I32),
                        pltpu.SemaphoreType.DMA],
    )
    return pl.pallas_call(
        functools.partial(_dispatch_kernel, bm=bm, tile_starts=tile_starts, n_tiles=sum(tiles)),
        grid_spec=grid_spec,
        out_shape=jax.ShapeDtypeStruct((n_rows_padded, PACKED), U32),
        compiler_params=pltpu.CompilerParams(
            dimension_semantics=("arbitrary",),
            vmem_limit_bytes=VMEM_LIMIT_BYTES),
        name="dispatch",
    )(block_valid, meta, post, *hns)


def _expert_kernel(be_ref, bv_ref, xs_ref, wgu_ref, bgu_ref, wd_ref, bd_ref, y_ref,
                   wgu_bf_ref, wd_bf_ref, *, bm):
    b = pl.program_id(0)
    valid = bv_ref[b]

    fresh = jnp.logical_or(b == 0, be_ref[b] != be_ref[jnp.maximum(b - 1, 0)])

    @pl.when(jnp.logical_and(valid > 0, fresh))
    def _():
        wgu_bf_ref[...] = wgu_ref[...].astype(BF16)
        wd_bf_ref[...] = wd_ref[...].astype(BF16)

    @pl.when(valid > 0)
    def _():
        row = lax.broadcasted_iota(I32, (bm, PACKED), 0)
        w = xs_ref[...]
        lo, hi = _unpack_rows(jnp.where(row < valid, w, jnp.zeros_like(w)))
        x = jnp.concatenate([lo, hi], axis=1)
        gu = jnp.dot(x, wgu_bf_ref[...], preferred_element_type=F32) + bgu_ref[...]
        glu = jnp.minimum(gu[:, :D_FF], SWIGLU_LIMIT)
        lin = jnp.clip(gu[:, D_FF:], -SWIGLU_LIMIT, SWIGLU_LIMIT)
        act = glu * jax.nn.sigmoid(SWIGLU_ALPHA * glu) * (lin + 1.0)
        y = jnp.dot(act.astype(BF16), wd_bf_ref[...], preferred_element_type=F32) + bd_ref[...]
        y_ref[...] = _pack_rows(y.astype(BF16).astype(F32))

    @pl.when(valid <= 0)
    def _():
        y_ref[...] = jnp.zeros(y_ref.shape, U32)


def _experts(xs, block_expert, block_valid, wgu, bgu, wd, bd, bm):
    n_blocks = xs.shape[0] // bm
    grid_spec = pltpu.PrefetchScalarGridSpec(
        num_scalar_prefetch=2,
        grid=(n_blocks,),
        in_specs=[
            pl.BlockSpec((bm, PACKED), lambda b, be, bv: (b, 0)),
            pl.BlockSpec((None, D_MODEL, 2 * D_FF), lambda b, be, bv: (be[b], 0, 0)),
            pl.BlockSpec((None, 1, 2 * D_FF), lambda b, be, bv: (be[b], 0, 0)),
            pl.BlockSpec((None, D_FF, D_MODEL), lambda b, be, bv: (be[b], 0, 0)),
            pl.BlockSpec((None, 1, D_MODEL), lambda b, be, bv: (be[b], 0, 0)),
        ],
        out_specs=pl.BlockSpec((bm, PACKED), lambda b, be, bv: (b, 0)),
        scratch_shapes=[pltpu.VMEM((D_MODEL, 2 * D_FF), BF16), pltpu.VMEM((D_FF, D_MODEL), BF16)],
    )
    return pl.pallas_call(
        functools.partial(_expert_kernel, bm=bm),
        grid_spec=grid_spec,
        out_shape=jax.ShapeDtypeStruct(xs.shape, U32),
        compiler_params=pltpu.CompilerParams(
            dimension_semantics=("arbitrary",),
            vmem_limit_bytes=VMEM_LIMIT_BYTES),
        name="experts",
    )(block_expert, block_valid, xs, wgu, bgu, wd, bd)


def _combine_kernel(meta_ref, next_meta_ref, pos_ref, gate_ref, x1_ref, gfin_ref, y_ref, o_ref,
                    buf_ref, sems, *, n_tiles):
    i = pl.program_id(0)
    cur = i % 2

    def fetch(m_ref, which):
        def issue(j, carry):
            dst = pl.multiple_of(m_ref[0, 1 + j], RUN_CHUNK)
            src = pl.multiple_of(m_ref[0, 1 + MAX_COPIES + j], ROW_ALIGN)
            _run_copy(y_ref, src, buf_ref.at[which], dst, sems.at[which]).start()
            return carry
        lax.fori_loop(0, m_ref[0, 0], issue, 0)

    @pl.when(i == 0)
    def _():
        buf_ref[...] = jnp.zeros(buf_ref.shape, U32)
        fetch(meta_ref, 0)

    @pl.when(i + 1 < n_tiles)
    def _():
        fetch(next_meta_ref, 1 - cur)

    def drain(j, carry):
        _run_copy(y_ref, 0, buf_ref.at[cur], 0, sems.at[cur]).wait()
        return carry

    lax.fori_loop(0, meta_ref[0, 0], drain, 0)

    slot = lax.broadcasted_iota(I32, (TOKEN_TILE, SORTED_ROWS), 1)
    pos = pos_ref[...]
    gates = gate_ref[...]
    g = jnp.zeros((TOKEN_TILE, SORTED_ROWS), F32)
    for k in range(TOP_K):
        g = jnp.where(slot == pos[:, k:k + 1], gates[:, k:k + 1], g)
    g = g.astype(BF16)
    lo, hi = _unpack_rows(buf_ref[cur])
    moe = jnp.concatenate([jnp.dot(g, lo, preferred_element_type=F32),
                           jnp.dot(g, hi, preferred_element_type=F32)], axis=1)
    o_ref[...] = _rms(x1_ref[...] + moe, gfin_ref[...])


def _combine(yrows, meta, pos, gates, x1, g_final):
    tokens = x1.shape[0]
    n_tiles = tokens // TOKEN_TILE
    meta_spec = lambda ahead: pl.BlockSpec(
        (None, 1, META_WIDTH), lambda i: (jnp.minimum(i + ahead, n_tiles - 1), 0, 0),
        memory_space=pltpu.SMEM)
    grid_spec = pltpu.PrefetchScalarGridSpec(
        num_scalar_prefetch=0,
        grid=(n_tiles,),
        in_specs=[
            meta_spec(0), meta_spec(1),
            pl.BlockSpec((TOKEN_TILE, TOP_K), lambda i: (i, 0)),
            pl.BlockSpec((TOKEN_TILE, TOP_K), lambda i: (i, 0)),
            pl.BlockSpec((TOKEN_TILE, D_MODEL), lambda i: (i, 0)),
            pl.BlockSpec((1, D_MODEL), lambda i: (0, 0)),
            pl.BlockSpec(memory_space=pl.ANY),
        ],
        out_specs=pl.BlockSpec((TOKEN_TILE, D_MODEL), lambda i: (i, 0)),
        scratch_shapes=[pltpu.VMEM((2, SORTED_ROWS, PACKED), U32),
                        pltpu.SemaphoreType.DMA((2,))],
    )
    return pl.pallas_call(
        functools.partial(_combine_kernel, n_tiles=n_tiles),
        grid_spec=grid_spec,
        out_shape=jax.ShapeDtypeStruct((tokens, D_MODEL), F32),
        compiler_params=pltpu.CompilerParams(
            dimension_semantics=("arbitrary",),
            vmem_limit_bytes=VMEM_LIMIT_BYTES),
        name="combine",
    )(meta, meta, pos, gates, x1, g_final, yrows)


def _route(logits, bm):
    tokens = logits.shape[0]
    n_tiles = tokens // TOKEN_TILE
    spare = RUN_CHUNK + bm - 1
    max_rows = tokens * TOP_K + n_tiles * N_EXPERTS * (ROW_ALIGN - 1) + N_EXPERTS * spare
    n_blocks = -(-max_rows // bm)
    top_val, top_idx = lax.top_k(logits, TOP_K)
    gates = jax.nn.softmax(top_val, axis=-1)
    onehot = top_idx[:, :, None] == jnp.arange(N_EXPERTS, dtype=top_idx.dtype)
    sel = jnp.sum(onehot, axis=1, dtype=F32).reshape(n_tiles, TOKEN_TILE, N_EXPERTS)
    earlier = jnp.tril(jnp.ones((TOKEN_TILE, TOKEN_TILE), F32), -1)
    rank = jnp.einsum('ts,nse->nte', earlier, sel).astype(I32)
    runs = jnp.sum(sel, axis=1).astype(I32)
    runs_aligned = (runs + ROW_ALIGN - 1) // ROW_ALIGN * ROW_ALIGN
    base = jnp.cumsum(runs_aligned, axis=0) - runs_aligned
    counts = jnp.sum(runs_aligned, axis=0)
    padded_counts = (counts + spare) // bm * bm
    padded_end = jnp.cumsum(padded_counts)
    padded_start = padded_end - padded_counts
    runs_padded = (runs + RUN_CHUNK - 1) // RUN_CHUNK * RUN_CHUNK
    seg = jnp.cumsum(runs_padded, axis=1) - runs_padded
    slot_of = (seg[:, None, :] + rank).reshape(tokens, 1, N_EXPERTS)
    pos = jnp.sum(jnp.where(onehot, slot_of, 0), axis=-1).astype(I32)

    n_chunks = runs_padded // RUN_CHUNK
    chunk_end = jnp.cumsum(n_chunks, axis=1)
    j = jnp.arange(MAX_COPIES, dtype=I32)
    chunk_start = chunk_end - n_chunks
    owns = jnp.logical_and(j[None, :, None] >= chunk_start[:, None, :],
                           j[None, :, None] < chunk_end[:, None, :])
    pick = lambda a: jnp.sum(jnp.where(owns, a[:, None, :], 0), axis=-1)
    within = (j[None, :] - pick(chunk_start)) * RUN_CHUNK
    live = j[None, :] < chunk_end[:, -1:]
    src = jnp.where(live, pick(seg) + within, 0)
    dst = jnp.where(live, pick(padded_start[None, :] + base) + within, 0)
    meta = jnp.concatenate([chunk_end[:, -1:], src, dst], axis=1).astype(I32)
    meta = jnp.pad(meta, ((0, 0), (0, META_WIDTH - meta.shape[1]))).reshape(n_tiles, 1, META_WIDTH)

    block_start = jnp.arange(n_blocks, dtype=I32) * bm
    in_region = jnp.logical_and(block_start[:, None] >= padded_start[None, :],
                                block_start[:, None] < padded_end[None, :])
    expert_ids = jnp.arange(N_EXPERTS, dtype=I32)
    block_expert = jnp.where(jnp.any(in_region, axis=1),
                             jnp.sum(jnp.where(in_region, expert_ids[None, :], 0), axis=1),
                             N_EXPERTS - 1).astype(I32)
    rows_left = jnp.sum(jnp.where(in_region, (padded_start + counts)[None, :], 0), axis=1) - block_start
    block_valid = jnp.clip(rows_left, 0, bm)
    post = pos.reshape(n_tiles, TOKEN_TILE, TOP_K).transpose(0, 2, 1)
    return gates, pos, post, meta, block_expert, block_valid.astype(I32), n_blocks


def _token_mixer(x, p):
    batch, seq, _ = x.shape
    t = _tiles(batch, seq)
    x2 = x.reshape(batch * seq, D_MODEL)
    proj = _inproj(x2, p["g_mix"], p["w_in"], p["g_sgu"], t["tm_in"])
    att = _attention(proj.reshape(batch, seq, IN_COLS), p["attn_scalars"], p["g_subln"],
                     t["tq"], t["tk"])
    return _mix(proj, att.reshape(batch * seq, D_MODEL), x2, p["w_spatial"], p["b_spatial"],
                p["w_branch_sgu"], p["w_branch_attn"], p["w_out"], p["g_ffn"],
                p["w_router"], p["b_router"], t["tm_mix"])


def _moe(mixed, p):
    bm = ROUTE_BLOCK
    logits = jnp.concatenate([lg[:, :N_EXPERTS] for _, _, lg in mixed], axis=0)
    gates, pos, post, meta, block_expert, block_valid, n_blocks = _route(logits, bm)
    xs = _dispatch([hn for _, hn, _ in mixed], meta, post, block_valid, n_blocks * bm, bm)
    tile0, spans = 0, []
    for x1, _, _ in mixed:
        spans.append((tile0, tile0 + x1.shape[0] // TOKEN_TILE))
        tile0 = spans[-1][1]
    yrows = _experts(xs, block_expert, block_valid, p["w_gate_up"], p["b_gate_up"],
                     p["w_down"], p["b_down"], bm)
    outs = []
    for (x1, _, _), (t0, t1) in zip(mixed, spans):
        tok = slice(t0 * TOKEN_TILE, t1 * TOKEN_TILE)
        outs.append(_combine(yrows, meta[t0:t1], pos[tok], gates[tok], x1, p["g_final"]))
    return outs


def _prepare_params(g_mix, w_in, g_sgu, w_spatial, b_spatial, lambda_q1, lambda_k1, lambda_q2,
                    lambda_k2, g_subln, w_branch_sgu, w_branch_attn, w_out, g_ffn, w_router,
                    b_router, w_gate_up, b_gate_up, w_down, b_down, g_final):
    lam = (jnp.exp(jnp.sum(lambda_q1[0].astype(F32) * lambda_k1[0].astype(F32)))
           - jnp.exp(jnp.sum(lambda_q2[0].astype(F32) * lambda_k2[0].astype(F32)))
           + LAMBDA_INIT)
    slopes = jnp.exp2(-8.0 * jnp.arange(1, ATT_HEADS + 1, dtype=F32) / ATT_HEADS) * LOG2E
    pad = LANES - N_EXPERTS
    return dict(
        g_mix=g_mix[0].reshape(1, D_MODEL),
        w_in=w_in[0].astype(BF16),
        g_sgu=g_sgu[0].reshape(1, D_MODEL),
        w_spatial=w_spatial[0].astype(BF16),
        b_spatial=jnp.broadcast_to(b_spatial[0][:, :, None], (SGU_GROUPS, CHUNK, CHUNK)).astype(F32),
        attn_scalars=jnp.concatenate([slopes, lam.reshape(1)]).astype(F32),
        g_subln=g_subln[0].reshape(1, ATT_VDIM),
        w_branch_sgu=w_branch_sgu[0].astype(BF16),
        w_branch_attn=w_branch_attn[0].astype(BF16),
        w_out=w_out[0].astype(BF16),
        g_ffn=g_ffn[0].reshape(1, D_MODEL),
        w_router=jnp.pad(w_router[0], ((0, 0), (0, pad))).astype(BF16),
        b_router=jnp.pad(b_router[0], (0, pad)).reshape(1, LANES).astype(F32),
        w_gate_up=w_gate_up[0],
        b_gate_up=b_gate_up[0].reshape(N_EXPERTS, 1, 2 * D_FF),
        w_down=w_down[0],
        b_down=b_down[0].reshape(N_EXPERTS, 1, D_MODEL),
        g_final=g_final.reshape(1, D_MODEL),
    )


def _encode(xs, p):
    outs = _moe([_token_mixer(x, p) for x in xs], p)
    return tuple(o.reshape(x.shape) for o, x in zip(outs, xs))


def kernel(x_prompt, x_sample, g_mix, w_in, g_sgu, w_spatial, b_spatial, lambda_q1, lambda_k1, lambda_q2, lambda_k2, g_subln, w_branch_sgu, w_branch_attn, w_out, g_ffn, w_router, b_router, w_gate_up, b_gate_up, w_down, b_down, g_final):
    p = _prepare_params(g_mix, w_in, g_sgu, w_spatial, b_spatial, lambda_q1, lambda_k1, lambda_q2,
                        lambda_k2, g_subln, w_branch_sgu, w_branch_attn, w_out, g_ffn, w_router,
                        b_router, w_gate_up, b_gate_up, w_down, b_down, g_final)
    return _encode((x_prompt, x_sample), p)
```

```python
import functools
import math

import jax
import jax.numpy as jnp
from jax import lax
from jax.experimental import pallas as pl
from jax.experimental.pallas import tpu as pltpu

D_MODEL = 1024
SGU_GROUPS = 8
CHUNK = 128
ATT_HEADS = 8
ATT_HEAD_DIM = 64
ATT_VDIM = 2 * ATT_HEAD_DIM
N_EXPERTS = 32
TOP_K = 4
D_FF = D_MODEL
SWIGLU_LIMIT = 7.0
SWIGLU_ALPHA = 1.702
RMS_EPS = 1e-5
LAMBDA_INIT = 0.8 - 0.6 * math.exp(-0.3 * 0)
LOG2E = 1.4426950408889634

COL_U, COL_V, COL_Q, COL_K, COL_VA, COL_GS, COL_GA = range(7)
IN_COLS = 7 * D_MODEL
LANES = 128
HEAD_BLOCKS = D_MODEL // LANES

VMEM_LIMIT_BYTES = 56 * 1024 * 1024

F32 = jnp.float32
BF16 = jnp.bfloat16
U32 = jnp.uint32
I32 = jnp.int32


def _tiles(batch, seq):
    tokens = batch * seq
    return dict(
        tm_in=min(1024, tokens),
        tq=min(256, seq),
        tk=min(512, seq),
        tm_mix=min(512, tokens),
    )


MATMUL_ROWS = 256
ROUTE_BLOCK = 512
TOKEN_TILE = 256
RUN_CHUNK = 16
ROW_ALIGN = 8
DMA_PRIORITIES = 2
SORTED_ROWS = -(-(TOKEN_TILE * TOP_K + N_EXPERTS * (RUN_CHUNK - 1)) // LANES) * LANES
MAX_COPIES = SORTED_ROWS // RUN_CHUNK
META_WIDTH = -(-(1 + 2 * MAX_COPIES) // LANES) * LANES
PACKED = D_MODEL // 2
HI_MASK = 0xFFFF0000


def _gelu_tanh(x):
    return 0.5 * x * (1.0 + jnp.tanh(math.sqrt(2.0 / math.pi) * (x + 0.044715 * (x * x * x))))


def _rms(x, g):
    r = x * lax.rsqrt(jnp.mean(x * x, axis=-1, keepdims=True) + RMS_EPS)
    return r * g


def _pack_rows(x):
    lo = lax.bitcast_convert_type(x[:, :PACKED], U32) >> jnp.uint32(16)
    hi = lax.bitcast_convert_type(x[:, PACKED:], U32) & jnp.uint32(HI_MASK)
    return lo | hi


def _unpack_rows(w):
    lo = lax.bitcast_convert_type(w << jnp.uint32(16), F32).astype(BF16)
    hi = lax.bitcast_convert_type(w & jnp.uint32(HI_MASK), F32).astype(BF16)
    return lo, hi


def _inproj_kernel(x_ref, gmix_ref, w_ref, gsgu_ref, o_ref, xn_ref, *, tm):
    j = pl.program_id(1)

    @pl.when(j == 0)
    def _():
        xn_ref[...] = _rms(x_ref[...], gmix_ref[...]).astype(BF16)

    def project(epilogue):
        for c in range(tm // MATMUL_ROWS):
            rows = slice(c * MATMUL_ROWS, (c + 1) * MATMUL_ROWS)
            y = jnp.dot(xn_ref[rows, :], w_ref[...], preferred_element_type=F32)
            o_ref[rows, :] = epilogue(y).astype(BF16)

    @pl.when(j == COL_U)
    def _():
        project(_gelu_tanh)

    @pl.when(j == COL_V)
    def _():
        project(lambda y: _rms(_gelu_tanh(y), gsgu_ref[...]))

    @pl.when(jnp.logical_and(j >= COL_Q, j <= COL_VA))
    def _():
        project(lambda y: y)

    @pl.when(j >= COL_GS)
    def _():
        project(jax.nn.sigmoid)


def _inproj(x2, g_mix, w_in, g_sgu, tm):
    tokens = x2.shape[0]
    return pl.pallas_call(
        functools.partial(_inproj_kernel, tm=tm),
        grid=(tokens // tm, IN_COLS // D_MODEL),
        in_specs=[
            pl.BlockSpec((tm, D_MODEL), lambda i, j: (i, 0)),
            pl.BlockSpec((1, D_MODEL), lambda i, j: (0, 0)),
            pl.BlockSpec((D_MODEL, D_MODEL), lambda i, j: (0, j)),
            pl.BlockSpec((1, D_MODEL), lambda i, j: (0, 0)),
        ],
        out_specs=pl.BlockSpec((tm, D_MODEL), lambda i, j: (i, j)),
        out_shape=jax.ShapeDtypeStruct((tokens, IN_COLS), BF16),
        scratch_shapes=[pltpu.VMEM((tm, D_MODEL), BF16)],
        compiler_params=pltpu.CompilerParams(
            dimension_semantics=("arbitrary", "arbitrary"),
            vmem_limit_bytes=VMEM_LIMIT_BYTES),
        name="inproj",
    )(x2, g_mix, w_in, g_sgu)


def _attn_kernel(sc_ref, q_ref, k_ref, v_ref, g_ref, o_ref,
                 qs_ref, tbl_ref, sb_ref, pm_ref, m_ref, l_ref, acc_ref, *, tq, tk, seq):
    h = pl.program_id(1)
    qi = pl.program_id(2)
    slope = sc_ref[h]
    lam = sc_ref[ATT_HEADS]
    nk = seq // tk
    n_diag = tk // tq
    rows = 2 * tq
    lane_tiles = tk // LANES

    @pl.when(qi == 0)
    def _():
        r = lax.broadcasted_iota(jnp.int32, (rows, tk), 0)
        c = lax.broadcasted_iota(jnp.int32, (rows, tk), 1)
        d0 = (jnp.where(r >= tq, r - tq, r) - c).astype(F32)
        tbl_ref[0] = slope * d0
        tbl_ref[1] = -slope * d0
        for d in range(n_diag):
            tbl_ref[2 + d] = slope * jnp.abs(d0 + float(d * tq))

    q = (q_ref[...].astype(F32) * (ATT_HEAD_DIM ** -0.5 * LOG2E)).astype(BF16)
    lane = lax.broadcasted_iota(jnp.int32, (tq, LANES), 1)
    zero = jnp.zeros_like(q)
    qs_ref[0:tq, :] = jnp.where(lane < ATT_HEAD_DIM, q, zero)
    qs_ref[tq:rows, :] = jnp.where(lane >= ATT_HEAD_DIM, q, zero)

    m_ref[...] = jnp.full(m_ref.shape, -1e30, F32)
    l_ref[...] = jnp.zeros(l_ref.shape, F32)
    acc_ref[...] = jnp.zeros(acc_ref.shape, F32)
    q0 = qi * tq
    kd = q0 // tk

    def tile_consts(kj):
        delta = (q0 - kj * tk).astype(F32)
        left, right = kj < kd, kj > kd
        sel = jnp.where(left, 0, jnp.where(right, 1, 2 + (q0 - kd * tk) // tq))
        cst = jnp.where(left, slope * delta, jnp.where(right, -slope * delta, 0.0))
        return sel, cst

    def scores(kj, slot):
        sel, _ = tile_consts(kj)
        kt = k_ref[pl.ds(pl.multiple_of(kj * tk, tk), tk), :]
        s = lax.dot_general(qs_ref[...], kt, (((1,), (1,)), ((), ())),
                            preferred_element_type=F32)
        pm = None
        for c in range(lane_tiles):
            cols = slice(c * LANES, (c + 1) * LANES)
            sb = s[:, cols] - tbl_ref[sel, :, cols]
            sb_ref[slot, :, cols] = sb
            pm = sb if pm is None else jnp.maximum(pm, sb)
        pm_ref[slot] = pm

    def accumulate(kj, slot):
        _, cst = tile_consts(kj)
        red = jnp.max(pm_ref[slot], axis=-1, keepdims=True) - cst
        m_old = m_ref[...]
        m_new = jnp.maximum(m_old, jnp.broadcast_to(red, m_old.shape))
        alpha = jnp.exp2(m_old - m_new)
        m_ref[...] = m_new
        shift = m_new + cst
        ps = [jnp.exp2(sb_ref[slot, :, c * LANES:(c + 1) * LANES] - shift).astype(BF16)
              for c in range(lane_tiles)]
        vt = v_ref[pl.ds(pl.multiple_of(kj * tk, tk), tk), :]
        v1 = jnp.concatenate([vt, jnp.ones_like(vt)], axis=1)
        pv = jnp.dot(jnp.concatenate(ps, axis=1), v1, preferred_element_type=F32)
        acc_ref[...] = alpha * acc_ref[...] + pv[:, :LANES]
        l_ref[...] = alpha * l_ref[...] + pv[:, LANES:]

    scores(0, 0)
    for kj in range(nk):
        if kj + 1 < nk:
            scores(kj + 1, (kj + 1) % 2)
        accumulate(kj, kj % 2)

    o = acc_ref[...] / l_ref[...]
    d = o[0:tq, :] - lam * o[tq:rows, :]
    o_ref[...] = (_rms(d, g_ref[...]) * (1.0 - LAMBDA_INIT)).astype(BF16)


def _attention(proj3, scal, g_subln, tq, tk):
    batch, seq, _ = proj3.shape
    kern = functools.partial(_attn_kernel, tq=tq, tk=tk, seq=seq)
    qb, kb, vb = COL_Q * HEAD_BLOCKS, COL_K * HEAD_BLOCKS, COL_VA * HEAD_BLOCKS
    return pl.pallas_call(
        kern,
        grid=(batch, ATT_HEADS, seq // tq),
        in_specs=[
            pl.BlockSpec(memory_space=pltpu.SMEM),
            pl.BlockSpec((None, tq, LANES), lambda b, h, i: (b, i, qb + h)),
            pl.BlockSpec((None, seq, LANES), lambda b, h, i: (b, 0, kb + h)),
            pl.BlockSpec((None, seq, LANES), lambda b, h, i: (b, 0, vb + h)),
            pl.BlockSpec((1, LANES), lambda b, h, i: (0, 0)),
        ],
        out_specs=pl.BlockSpec((None, tq, LANES), lambda b, h, i: (b, i, h)),
        out_shape=jax.ShapeDtypeStruct((batch, seq, D_MODEL), BF16),
        scratch_shapes=[
            pltpu.VMEM((2 * tq, LANES), BF16),
            pltpu.VMEM((2 + tk // tq, 2 * tq, tk), F32),
            pltpu.VMEM((2, 2 * tq, tk), F32),
            pltpu.VMEM((2, 2 * tq, LANES), F32),
            pltpu.VMEM((2 * tq, LANES), F32),
            pltpu.VMEM((2 * tq, LANES), F32),
            pltpu.VMEM((2 * tq, LANES), F32),
        ],
        compiler_params=pltpu.CompilerParams(
            dimension_semantics=("arbitrary", "arbitrary", "arbitrary"),
            vmem_limit_bytes=VMEM_LIMIT_BYTES),
        name="diffattn",
    )(scal, proj3, proj3, proj3, g_subln)


def _mix_kernel(u_ref, vn_ref, gs_ref, ga_ref, att_ref, x_ref, ws_ref, bs_ref,
                wbs_ref, wba_ref, wo_ref, gffn_ref, wr_ref, br_ref,
                x1_ref, hn_ref, lg_ref, sgu_ref, *, tm):
    for n in range(tm // CHUNK):
        rows = slice(n * CHUNK, (n + 1) * CHUNK)
        for g in range(SGU_GROUPS):
            cols = slice(g * CHUNK, (g + 1) * CHUNK)
            vm = jnp.dot(ws_ref[g], vn_ref[rows, cols], preferred_element_type=F32) + bs_ref[g]
            sgu_ref[rows, cols] = (u_ref[rows, cols].astype(F32) * vm).astype(BF16)

    for c in range(tm // MATMUL_ROWS):
        rows = slice(c * MATMUL_ROWS, (c + 1) * MATMUL_ROWS)
        a = jnp.dot(sgu_ref[rows, :], wbs_ref[...], preferred_element_type=F32)
        b = jnp.dot(att_ref[rows, :], wba_ref[...], preferred_element_type=F32)
        merged = gs_ref[rows, :].astype(F32) * a + ga_ref[rows, :].astype(F32) * b
        x1 = x_ref[rows, :] + jnp.dot(merged.astype(BF16), wo_ref[...],
                                      preferred_element_type=F32)
        x1_ref[rows, :] = x1
        hn = _rms(x1, gffn_ref[...]).astype(BF16)
        hn_ref[rows, :] = hn
        lg_ref[rows, :] = jnp.dot(hn, wr_ref[...], preferred_element_type=F32) + br_ref[...]


def _mix(proj, att, x2, ws, bs, wbs, wba, wo, g_ffn, wr, br, tm):
    tokens = x2.shape[0]
    row = lambda c: pl.BlockSpec((tm, D_MODEL), lambda i: (i, c))
    full = lambda shape: pl.BlockSpec(shape, lambda i: (0,) * len(shape))
    return pl.pallas_call(
        functools.partial(_mix_kernel, tm=tm),
        grid=(tokens // tm,),
        in_specs=[
            row(COL_U), row(COL_V), row(COL_GS), row(COL_GA),
            row(0), row(0),
            full((SGU_GROUPS, CHUNK, CHUNK)), full((SGU_GROUPS, CHUNK, CHUNK)),
            full((D_MODEL, D_MODEL)), full((D_MODEL, D_MODEL)), full((D_MODEL, D_MODEL)),
            full((1, D_MODEL)), full((D_MODEL, LANES)), full((1, LANES)),
        ],
        out_specs=[row(0), row(0), pl.BlockSpec((tm, LANES), lambda i: (i, 0))],
        out_shape=[
            jax.ShapeDtypeStruct((tokens, D_MODEL), F32),
            jax.ShapeDtypeStruct((tokens, D_MODEL), BF16),
            jax.ShapeDtypeStruct((tokens, LANES), F32),
        ],
        scratch_shapes=[pltpu.VMEM((tm, D_MODEL), BF16)],
        compiler_params=pltpu.CompilerParams(
            dimension_semantics=("arbitrary",),
            vmem_limit_bytes=VMEM_LIMIT_BYTES),
        name="mix",
    )(proj, proj, proj, proj, att, x2, ws, bs, wbs, wba, wo, g_ffn, wr, br)


def _run_copy(src_ref, src_row, dst_ref, dst_row, sem):
    return pltpu.make_async_copy(src_ref.at[pl.ds(src_row, RUN_CHUNK)],
                                 dst_ref.at[pl.ds(dst_row, RUN_CHUNK)], sem)


def _dispatch_kernel(bv_ref, meta_ref, post_ref, *rest, bm, tile_starts, n_tiles):
    hn_refs = rest[:len(tile_starts)]
    xs_ref, srt_ref, zero_ref, pending_ref, sem = rest[len(tile_starts):]
    i = pl.program_id(0)

    @pl.when(i == 0)
    def _():
        zero_ref[...] = jnp.zeros(zero_ref.shape, U32)

        def maybe_zero(b, carry):
            @pl.when(bv_ref[b] < bm)
            def _():
                cp = pltpu.make_async_copy(
                    zero_ref, xs_ref.at[pl.ds(pl.multiple_of(b * bm, bm), bm)], sem)
                cp.start()
                cp.wait()
            return carry

        lax.fori_loop(0, xs_ref.shape[0] // bm, maybe_zero, 0)

    hn = hn_refs[0][...]
    for ref, start in zip(hn_refs[1:], tile_starts[1:]):
        hn = jnp.where(i >= start, ref[...], hn)

    slot = lax.broadcasted_iota(I32, (SORTED_ROWS, TOKEN_TILE), 0)
    hit = slot == post_ref[0:1, :]
    for k in range(1, TOP_K):
        hit = jnp.logical_or(hit, slot == post_ref[k:k + 1, :])
    onehot = jnp.where(hit, 1.0, 0.0).astype(BF16)
    buf = srt_ref.at[i % 2]
    buf[...] = _pack_rows(jnp.dot(onehot, hn, preferred_element_type=F32))

    def drain(count):
        def wait_one(j, carry):
            _run_copy(buf, 0, xs_ref, 0, sem).wait()
            return carry
        lax.fori_loop(0, count, wait_one, 0)

    @pl.when(i > 0)
    def _():
        drain(pending_ref[0])

    n_copies = meta_ref[0, 0]

    def issue(pair, carry):
        for priority in range(DMA_PRIORITIES):
            j = pair * DMA_PRIORITIES + priority

            @pl.when(j < n_copies)
            def _():
                src = pl.multiple_of(meta_ref[0, 1 + j], RUN_CHUNK)
                dst = pl.multiple_of(meta_ref[0, 1 + MAX_COPIES + j], ROW_ALIGN)
                _run_copy(buf, src, xs_ref, dst, sem).start(priority=priority)
        return carry

    lax.fori_loop(0, (n_copies + DMA_PRIORITIES - 1) // DMA_PRIORITIES, issue, 0)
    pending_ref[0] = n_copies

    @pl.when(i == n_tiles - 1)
    def _():
        drain(n_copies)


def _dispatch(hns, meta, post, block_valid, n_rows_padded, bm):
    tiles = [hn.shape[0] // TOKEN_TILE for hn in hns]
    tile_starts = tuple(sum(tiles[:k]) for k in range(len(tiles)))

    def hn_spec(start, count):
        return pl.BlockSpec((TOKEN_TILE, D_MODEL),
                            lambda i, bv: (jnp.clip(i - start, 0, count - 1), 0))

    in_specs = [
        pl.BlockSpec((None, 1, META_WIDTH), lambda i, bv: (i, 0, 0), memory_space=pltpu.SMEM),
        pl.BlockSpec((None, TOP_K, TOKEN_TILE), lambda i, bv: (i, 0, 0)),
    ] + [hn_spec(s, c) for s, c in zip(tile_starts, tiles)]
    grid_spec = pltpu.PrefetchScalarGridSpec(
        num_scalar_prefetch=1,
        grid=(sum(tiles),),
        in_specs=in_specs,
        out_specs=pl.BlockSpec(memory_space=pl.ANY),
        scratch_shapes=[pltpu.VMEM((2, SORTED_ROWS, PACKED), U32),
                        pltpu.VMEM((bm, PACKED), U32),
                        pltpu.SMEM((1,), I32),
                        pltpu.SemaphoreType.DMA],
    )
    return pl.pallas_call(
        functools.partial(_dispatch_kernel, bm=bm, tile_starts=tile_starts, n_tiles=sum(tiles)),
        grid_spec=grid_spec,
        out_shape=jax.ShapeDtypeStruct((n_rows_padded, PACKED), U32),
        compiler_params=pltpu.CompilerParams(
            dimension_semantics=("arbitrary",),
            vmem_limit_bytes=VMEM_LIMIT_BYTES),
        name="dispatch",
    )(block_valid, meta, post, *hns)


def _expert_kernel(be_ref, bv_ref, xs_ref, wgu_ref, bgu_ref, wd_ref, bd_ref, y_ref,
                   wgu_bf_ref, wd_bf_ref, *, bm):
    b = pl.program_id(0)
    valid = bv_ref[b]

    fresh = jnp.logical_or(b == 0, be_ref[b] != be_ref[jnp.maximum(b - 1, 0)])

    @pl.when(jnp.logical_and(valid > 0, fresh))
    def _():
        wgu_bf_ref[...] = wgu_ref[...].astype(BF16)
        wd_bf_ref[...] = wd_ref[...].astype(BF16)

    @pl.when(valid > 0)
    def _():
        row = lax.broadcasted_iota(I32, (bm, PACKED), 0)
        w = xs_ref[...]
        lo, hi = _unpack_rows(jnp.where(row < valid, w, jnp.zeros_like(w)))
        x = jnp.concatenate([lo, hi], axis=1)
        gu = jnp.dot(x, wgu_bf_ref[...], preferred_element_type=F32) + bgu_ref[...]
        glu = jnp.minimum(gu[:, :D_FF], SWIGLU_LIMIT)
        lin = jnp.clip(gu[:, D_FF:], -SWIGLU_LIMIT, SWIGLU_LIMIT)
        act = glu * jax.nn.sigmoid(SWIGLU_ALPHA * glu) * (lin + 1.0)
        y = jnp.dot(act.astype(BF16), wd_bf_ref[...], preferred_element_type=F32) + bd_ref[...]
        y_ref[...] = _pack_rows(y.astype(BF16).astype(F32))

    @pl.when(valid <= 0)
    def _():
        y_ref[...] = jnp.zeros(y_ref.shape, U32)


def _experts(xs, block_expert, block_valid, wgu, bgu, wd, bd, bm):
    n_blocks = xs.shape[0] // bm
    grid_spec = pltpu.PrefetchScalarGridSpec(
        num_scalar_prefetch=2,
        grid=(n_blocks,),
        in_specs=[
            pl.BlockSpec((bm, PACKED), lambda b, be, bv: (b, 0)),
            pl.BlockSpec((None, D_MODEL, 2 * D_FF), lambda b, be, bv: (be[b], 0, 0)),
            pl.BlockSpec((None, 1, 2 * D_FF), lambda b, be, bv: (be[b], 0, 0)),
            pl.BlockSpec((None, D_FF, D_MODEL), lambda b, be, bv: (be[b], 0, 0)),
            pl.BlockSpec((None, 1, D_MODEL), lambda b, be, bv: (be[b], 0, 0)),
        ],
        out_specs=pl.BlockSpec((bm, PACKED), lambda b, be, bv: (b, 0)),
        scratch_shapes=[pltpu.VMEM((D_MODEL, 2 * D_FF), BF16), pltpu.VMEM((D_FF, D_MODEL), BF16)],
    )
    return pl.pallas_call(
        functools.partial(_expert_kernel, bm=bm),
        grid_spec=grid_spec,
        out_shape=jax.ShapeDtypeStruct(xs.shape, U32),
        compiler_params=pltpu.CompilerParams(
            dimension_semantics=("arbitrary",),
            vmem_limit_bytes=VMEM_LIMIT_BYTES),
        name="experts",
    )(block_expert, block_valid, xs, wgu, bgu, wd, bd)


def _combine_kernel(meta_ref, next_meta_ref, pos_ref, gate_ref, x1_ref, gfin_ref, y_ref, o_ref,
                    buf_ref, sems, *, n_tiles):
    i = pl.program_id(0)
    cur = i % 2

    def fetch(m_ref, which):
        count = m_ref[0, 0]

        def issue(pair, carry):
            for priority in range(DMA_PRIORITIES):
                j = pair * DMA_PRIORITIES + priority

                @pl.when(j < count)
                def _():
                    dst = pl.multiple_of(m_ref[0, 1 + j], RUN_CHUNK)
                    src = pl.multiple_of(m_ref[0, 1 + MAX_COPIES + j], ROW_ALIGN)
                    _run_copy(y_ref, src, buf_ref.at[which], dst,
                              sems.at[which]).start(priority=priority)
            return carry
        lax.fori_loop(0, (count + DMA_PRIORITIES - 1) // DMA_PRIORITIES, issue, 0)

    @pl.when(i == 0)
    def _():
        buf_ref[...] = jnp.zeros(buf_ref.shape, U32)
        fetch(meta_ref, 0)

    @pl.when(i + 1 < n_tiles)
    def _():
        fetch(next_meta_ref, 1 - cur)

    def drain(j, carry):
        _run_copy(y_ref, 0, buf_ref.at[cur], 0, sems.at[cur]).wait()
        return carry

    lax.fori_loop(0, meta_ref[0, 0], drain, 0)

    slot = lax.broadcasted_iota(I32, (TOKEN_TILE, SORTED_ROWS), 1)
    pos = pos_ref[...]
    gates = gate_ref[...]
    g = jnp.zeros((TOKEN_TILE, SORTED_ROWS), F32)
    for k in range(TOP_K):
        g = jnp.where(slot == pos[:, k:k + 1], gates[:, k:k + 1], g)
    g = g.astype(BF16)
    lo, hi = _unpack_rows(buf_ref[cur])
    moe = jnp.concatenate([jnp.dot(g, lo, preferred_element_type=F32),
                           jnp.dot(g, hi, preferred_element_type=F32)], axis=1)
    o_ref[...] = _rms(x1_ref[...] + moe, gfin_ref[...])


def _combine(yrows, meta, pos, gates, x1, g_final):
    tokens = x1.shape[0]
    n_tiles = tokens // TOKEN_TILE
    meta_spec = lambda ahead: pl.BlockSpec(
        (None, 1, META_WIDTH), lambda i: (jnp.minimum(i + ahead, n_tiles - 1), 0, 0),
        memory_space=pltpu.SMEM)
    grid_spec = pltpu.PrefetchScalarGridSpec(
        num_scalar_prefetch=0,
        grid=(n_tiles,),
        in_specs=[
            meta_spec(0), meta_spec(1),
            pl.BlockSpec((TOKEN_TILE, TOP_K), lambda i: (i, 0)),
            pl.BlockSpec((TOKEN_TILE, TOP_K), lambda i: (i, 0)),
            pl.BlockSpec((TOKEN_TILE, D_MODEL), lambda i: (i, 0)),
            pl.BlockSpec((1, D_MODEL), lambda i: (0, 0)),
            pl.BlockSpec(memory_space=pl.ANY),
        ],
        out_specs=pl.BlockSpec((TOKEN_TILE, D_MODEL), lambda i: (i, 0)),
        scratch_shapes=[pltpu.VMEM((2, SORTED_ROWS, PACKED), U32),
                        pltpu.SemaphoreType.DMA((2,))],
    )
    return pl.pallas_call(
        functools.partial(_combine_kernel, n_tiles=n_tiles),
        grid_spec=grid_spec,
        out_shape=jax.ShapeDtypeStruct((tokens, D_MODEL), F32),
        compiler_params=pltpu.CompilerParams(
            dimension_semantics=("arbitrary",),
            vmem_limit_bytes=VMEM_LIMIT_BYTES),
        name="combine",
    )(meta, meta, pos, gates, x1, g_final, yrows)


def _route(logits, bm):
    tokens = logits.shape[0]
    n_tiles = tokens // TOKEN_TILE
    spare = RUN_CHUNK + bm - 1
    max_rows = tokens * TOP_K + n_tiles * N_EXPERTS * (ROW_ALIGN - 1) + N_EXPERTS * spare
    n_blocks = -(-max_rows // bm)
    top_val, top_idx = lax.top_k(logits, TOP_K)
    gates = jax.nn.softmax(top_val, axis=-1)
    onehot = top_idx[:, :, None] == jnp.arange(N_EXPERTS, dtype=top_idx.dtype)
    sel = jnp.sum(onehot, axis=1, dtype=F32).reshape(n_tiles, TOKEN_TILE, N_EXPERTS)
    earlier = jnp.tril(jnp.ones((TOKEN_TILE, TOKEN_TILE), F32), -1)
    rank = jnp.einsum('ts,nse->nte', earlier, sel).astype(I32)
    runs = jnp.sum(sel, axis=1).astype(I32)
    runs_aligned = (runs + ROW_ALIGN - 1) // ROW_ALIGN * ROW_ALIGN
    base = jnp.cumsum(runs_aligned, axis=0) - runs_aligned
    counts = jnp.sum(runs_aligned, axis=0)
    padded_counts = (counts + spare) // bm * bm
    padded_end = jnp.cumsum(padded_counts)
    padded_start = padded_end - padded_counts
    runs_padded = (runs + RUN_CHUNK - 1) // RUN_CHUNK * RUN_CHUNK
    seg = jnp.cumsum(runs_padded, axis=1) - runs_padded
    slot_of = (seg[:, None, :] + rank).reshape(tokens, 1, N_EXPERTS)
    pos = jnp.sum(jnp.where(onehot, slot_of, 0), axis=-1).astype(I32)

    n_chunks = runs_padded // RUN_CHUNK
    chunk_end = jnp.cumsum(n_chunks, axis=1)
    j = jnp.arange(MAX_COPIES, dtype=I32)
    chunk_start = chunk_end - n_chunks
    owns = jnp.logical_and(j[None, :, None] >= chunk_start[:, None, :],
                           j[None, :, None] < chunk_end[:, None, :])
    pick = lambda a: jnp.sum(jnp.where(owns, a[:, None, :], 0), axis=-1)
    within = (j[None, :] - pick(chunk_start)) * RUN_CHUNK
    live = j[None, :] < chunk_end[:, -1:]
    src = jnp.where(live, pick(seg) + within, 0)
    dst = jnp.where(live, pick(padded_start[None, :] + base) + within, 0)
    meta = jnp.concatenate([chunk_end[:, -1:], src, dst], axis=1).astype(I32)
    meta = jnp.pad(meta, ((0, 0), (0, META_WIDTH - meta.shape[1]))).reshape(n_tiles, 1, META_WIDTH)

    block_start = jnp.arange(n_blocks, dtype=I32) * bm
    in_region = jnp.logical_and(block_start[:, None] >= padded_start[None, :],
                                block_start[:, None] < padded_end[None, :])
    expert_ids = jnp.arange(N_EXPERTS, dtype=I32)
    block_expert = jnp.where(jnp.any(in_region, axis=1),
                             jnp.sum(jnp.where(in_region, expert_ids[None, :], 0), axis=1),
                             N_EXPERTS - 1).astype(I32)
    rows_left = jnp.sum(jnp.where(in_region, (padded_start + counts)[None, :], 0), axis=1) - block_start
    block_valid = jnp.clip(rows_left, 0, bm)
    post = pos.reshape(n_tiles, TOKEN_TILE, TOP_K).transpose(0, 2, 1)
    return gates, pos, post, meta, block_expert, block_valid.astype(I32), n_blocks


def _token_mixer(x, p):
    batch, seq, _ = x.shape
    t = _tiles(batch, seq)
    x2 = x.reshape(batch * seq, D_MODEL)
    proj = _inproj(x2, p["g_mix"], p["w_in"], p["g_sgu"], t["tm_in"])
    att = _attention(proj.reshape(batch, seq, IN_COLS), p["attn_scalars"], p["g_subln"],
                     t["tq"], t["tk"])
    return _mix(proj, att.reshape(batch * seq, D_MODEL), x2, p["w_spatial"], p["b_spatial"],
                p["w_branch_sgu"], p["w_branch_attn"], p["w_out"], p["g_ffn"],
                p["w_router"], p["b_router"], t["tm_mix"])


def _moe(mixed, p):
    bm = ROUTE_BLOCK
    logits = jnp.concatenate([lg[:, :N_EXPERTS] for _, _, lg in mixed], axis=0)
    gates, pos, post, meta, block_expert, block_valid, n_blocks = _route(logits, bm)
    xs = _dispatch([hn for _, hn, _ in mixed], meta, post, block_valid, n_blocks * bm, bm)
    tile0, spans = 0, []
    for x1, _, _ in mixed:
        spans.append((tile0, tile0 + x1.shape[0] // TOKEN_TILE))
        tile0 = spans[-1][1]
    yrows = _experts(xs, block_expert, block_valid, p["w_gate_up"], p["b_gate_up"],
                     p["w_down"], p["b_down"], bm)
    outs = []
    for (x1, _, _), (t0, t1) in zip(mixed, spans):
        tok = slice(t0 * TOKEN_TILE, t1 * TOKEN_TILE)
        outs.append(_combine(yrows, meta[t0:t1], pos[tok], gates[tok], x1, p["g_final"]))
    return outs


def _prepare_params(g_mix, w_in, g_sgu, w_spatial, b_spatial, lambda_q1, lambda_k1, lambda_q2,
                    lambda_k2, g_subln, w_branch_sgu, w_branch_attn, w_out, g_ffn, w_router,
                    b_router, w_gate_up, b_gate_up, w_down, b_down, g_final):
    lam = (jnp.exp(jnp.sum(lambda_q1[0].astype(F32) * lambda_k1[0].astype(F32)))
           - jnp.exp(jnp.sum(lambda_q2[0].astype(F32) * lambda_k2[0].astype(F32)))
           + LAMBDA_INIT)
    slopes = jnp.exp2(-8.0 * jnp.arange(1, ATT_HEADS + 1, dtype=F32) / ATT_HEADS) * LOG2E
    pad = LANES - N_EXPERTS
    return dict(
        g_mix=g_mix[0].reshape(1, D_MODEL),
        w_in=w_in[0].astype(BF16),
        g_sgu=g_sgu[0].reshape(1, D_MODEL),
        w_spatial=w_spatial[0].astype(BF16),
        b_spatial=jnp.broadcast_to(b_spatial[0][:, :, None], (SGU_GROUPS, CHUNK, CHUNK)).astype(F32),
        attn_scalars=jnp.concatenate([slopes, lam.reshape(1)]).astype(F32),
        g_subln=g_subln[0].reshape(1, ATT_VDIM),
        w_branch_sgu=w_branch_sgu[0].astype(BF16),
        w_branch_attn=w_branch_attn[0].astype(BF16),
        w_out=w_out[0].astype(BF16),
        g_ffn=g_ffn[0].reshape(1, D_MODEL),
        w_router=jnp.pad(w_router[0], ((0, 0), (0, pad))).astype(BF16),
        b_router=jnp.pad(b_router[0], (0, pad)).reshape(1, LANES).astype(F32),
        w_gate_up=w_gate_up[0],
        b_gate_up=b_gate_up[0].reshape(N_EXPERTS, 1, 2 * D_FF),
        w_down=w_down[0],
        b_down=b_down[0].reshape(N_EXPERTS, 1, D_MODEL),
        g_final=g_final.reshape(1, D_MODEL),
    )


def _encode(xs, p):
    outs = _moe([_token_mixer(x, p) for x in xs], p)
    return tuple(o.reshape(x.shape) for o, x in zip(outs, xs))


def kernel(x_prompt, x_sample, g_mix, w_in, g_sgu, w_spatial, b_spatial, lambda_q1, lambda_k1, lambda_q2, lambda_k2, g_subln, w_branch_sgu, w_branch_attn, w_out, g_ffn, w_router, b_router, w_gate_up, b_gate_up, w_down, b_down, g_final):
    p = _prepare_params(g_mix, w_in, g_sgu, w_spatial, b_spatial, lambda_q1, lambda_k1, lambda_q2,
                        lambda_k2, g_subln, w_branch_sgu, w_branch_attn, w_out, g_ffn, w_router,
                        b_router, w_gate_up, b_gate_up, w_down, b_down, g_final)
    return _encode((x_prompt, x_sample), p)
```
